```python
import math
import jax
import jax.numpy as jnp
from jax import lax
import numpy as np

D_MODEL = 2048
BATCH = 4
SEQ = 4096
DEPTH = 2

MIX_WIDTH = D_MODEL
N_GROUPS = 4
GROUP_WIDTH = MIX_WIDTH // N_GROUPS
NORM_EPS = 1e-6
CONV_WIDTH = 4

RW_HEAD = 64
RW_HEADS = GROUP_WIDTH // RW_HEAD
RW_W_RANK = 64
RW_A_RANK = 64
RW_G_RANK = 128
RW_V_RANK = 32
RW_GN_EPS = 64e-5
RW_WIDTHS = (GROUP_WIDTH, GROUP_WIDTH, GROUP_WIDTH, RW_W_RANK, RW_A_RANK, RW_G_RANK)
RW_COLS = sum(RW_WIDTHS)

GLA_HEADS = 4
GLA_DV = GROUP_WIDTH // GLA_HEADS
GLA_DK = GLA_DV // 2
GLA_GATE_RANK = 16
GLA_GATE_NORMALIZER = 16.0
GLA_CHUNK = 64
GLA_WIDTHS = (GLA_HEADS * GLA_DK, GLA_HEADS * GLA_DK, GROUP_WIDTH, GROUP_WIDTH, GLA_GATE_RANK)
GLA_COLS = sum(GLA_WIDTHS)

LRU_WIDTH = GROUP_WIDTH
LRU_BLOCKS = 8
LRU_BLOCK = LRU_WIDTH // LRU_BLOCKS
LRU_C = 8.0
LRU_WIDTHS = (LRU_WIDTH, LRU_WIDTH)
LRU_COLS = sum(LRU_WIDTHS)

GDN_HEADS = 4
GDN_DK = 128
GDN_DV = GROUP_WIDTH // GDN_HEADS
GDN_CHUNK = 64
GDN_WIDTHS = (GDN_HEADS * GDN_DK, GDN_HEADS * GDN_DK, GDN_HEADS * GDN_DV, GDN_HEADS * GDN_DV, GDN_HEADS, GDN_HEADS)
GDN_COLS = sum(GDN_WIDTHS)
GDN_CONV_CH = 2 * GDN_HEADS * GDN_DK + GDN_HEADS * GDN_DV

IN_COLS = RW_COLS + GLA_COLS + LRU_COLS + GDN_COLS

PEER_HEADS = 8
PEER_KEYS = 128
PEER_EXPERTS = PEER_KEYS * PEER_KEYS
PEER_QDIM = 256
PEER_TOPK = 16
PEER_TOKEN_BLOCK = 128

kernel_name = "hybrid_rwkv7_gla_rglru_gdn_peer"


def rms_norm(x, gain, eps=NORM_EPS):
    xf = x.astype(jnp.float32)
    y = xf * lax.rsqrt(jnp.mean(xf * xf, axis=-1, keepdims=True) + eps)
    return (y * gain).astype(x.dtype)


def _split(z, widths):
    return jnp.split(z, np.cumsum(widths)[:-1].tolist(), axis=-1)


def _token_shift(x):
    return jnp.pad(x[:, :-1], ((0, 0), (1, 0), (0, 0)))


def _causal_conv(x, w):
    width, ch = w.shape
    return lax.conv_general_dilated(
        x, w[:, None, :].astype(x.dtype), window_strides=(1,), padding=[(width - 1, 0)],
        dimension_numbers=("NWC", "WIO", "NWC"), feature_group_count=ch)


def _l2norm(t, eps=1e-6):
    return t * lax.rsqrt(jnp.sum(t * t, axis=-1, keepdims=True) + eps)


def _to_chunks(t, chunk):
    b, s = t.shape[:2]
    t = t.reshape((b, s // chunk, chunk) + t.shape[2:])
    return jnp.moveaxis(t, (1, 3), (0, 2))


def _from_chunks(t):
    t = jnp.moveaxis(t, (0, 2), (1, 3))
    return t.reshape((t.shape[0], t.shape[1] * t.shape[2]) + t.shape[3:])


def _rwkv7_scan(r, w, k, v, a, b):
    bsz, _, nh, n = r.shape

    def step(s, inp):
        r_t, w_t, k_t, v_t, a_t, b_t = inp
        sa = jnp.einsum("bhvk,bhk->bhv", s, a_t)
        s = s * w_t[:, :, None, :] + sa[..., None] * b_t[:, :, None, :] + v_t[..., None] * k_t[:, :, None, :]
        return s, jnp.einsum("bhvk,bhk->bhv", s, r_t)

    xs = tuple(jnp.swapaxes(t, 0, 1) for t in (r, w, k, v, a, b))
    _, y = lax.scan(step, jnp.zeros((bsz, nh, n, n), jnp.float32), xs)
    return jnp.swapaxes(y, 0, 1)


def rwkv7_mixer(p, v_first, mu, w0, w_up, a0, a_up, g_up, k_k, k_a, r_k, ln_w, ln_b, v_res):
    bsz, s, _ = p.shape
    z = p + (_token_shift(p) - p) * mu
    r, k, v, wd, ad, gd = _split(z, RW_WIDTHS)
    w_raw = -jax.nn.softplus(-(w0 + jnp.tanh(wd) @ w_up)) - 0.5
    a = jax.nn.sigmoid(a0 + ad @ a_up)
    g = jax.nn.sigmoid(gd) @ g_up
    if v_res is None:
        v_first = v
    else:
        v0, v_down, v_up = v_res
        v = v + (v_first - v) * jax.nn.sigmoid(v0 + (v @ v_down) @ v_up)

    def heads(t):
        return t.reshape(bsz, s, RW_HEADS, RW_HEAD).astype(jnp.float32)

    kk = heads(k * k_k)
    kk = kk * lax.rsqrt(jnp.maximum(jnp.sum(kk * kk, axis=-1, keepdims=True), 1e-24))
    k = k * (1.0 + (a - 1.0) * k_a)
    rh, kh, vh, ah = heads(r), heads(k), heads(v), heads(a)
    decay = jnp.exp(-jnp.exp(heads(w_raw)))
    y = _rwkv7_scan(rh, decay, kh, vh, -kk, kk * ah)
    mean = jnp.mean(y, axis=-1, keepdims=True)
    var = jnp.mean(jnp.square(y - mean), axis=-1, keepdims=True)
    y = ((y - mean) * lax.rsqrt(var + RW_GN_EPS)).reshape(bsz, s, GROUP_WIDTH) * ln_w + ln_b
    bonus = (jnp.sum(rh * kh * r_k, axis=-1, keepdims=True) * vh).reshape(bsz, s, GROUP_WIDTH)
    return ((y + bonus) * g).astype(p.dtype), v_first


def _gla_chunked(q, k, v, log_a):
    bsz, s, nh, dk = q.shape
    dv = v.shape[-1]
    c = GLA_CHUNK
    q, k, v, log_a = (_to_chunks(t, c) for t in (q * dk ** -0.5, k, v, log_a))
    b = jnp.cumsum(log_a, axis=3)
    b_ref = b[:, :, :, c // 2 - 1:c // 2, :]
    b_last = b[:, :, :, -1:, :]
    causal = jnp.tril(jnp.ones((c, c), dtype=bool))
    scores = jnp.einsum("nbhik,nbhjk->nbhij", q * jnp.exp(b - b_ref), k * jnp.exp(b_ref - b))
    o_intra = jnp.einsum("nbhij,nbhjv->nbhiv", jnp.where(causal, scores, 0.0), v)
    q_in = q * jnp.exp(b)
    k_st = k * jnp.exp(b_last - b)
    dec = jnp.exp(b_last[:, :, :, 0, :])

    def step(st, inp):
        q_n, k_n, v_n, d_n = inp
        o = jnp.einsum("bhck,bhkv->bhcv", q_n, st)
        st = st * d_n[..., None] + jnp.einsum("bhck,bhcv->bhkv", k_n, v_n)
        return st, o

    _, o_inter = lax.scan(step, jnp.zeros((bsz, nh, dk, dv), jnp.float32), (q_in, k_st, v, dec))
    return _from_chunks(o_intra + o_inter)


def gla_mixer(p, a_up, a_bias, norm_gain):
    bsz, s, _ = p.shape
    q, k, v, g, ad = _split(p, GLA_WIDTHS)
    log_a = jax.nn.log_sigmoid((ad @ a_up + a_bias).astype(jnp.float32)) / GLA_GATE_NORMALIZER

    def hk(t):
        return t.reshape(bsz, s, GLA_HEADS, GLA_DK).astype(jnp.float32)

    o = _gla_chunked(hk(q), hk(k), v.reshape(bsz, s, GLA_HEADS, GLA_DV).astype(jnp.float32), hk(log_a))
    o = rms_norm(o, norm_gain).reshape(bsz, s, GROUP_WIDTH)
    return (o * jax.nn.silu(g.astype(jnp.float32))).astype(p.dtype)


def _linear_scan(a, u):
    def combine(c1, c2):
        a1, b1 = c1
        a2, b2 = c2
        return a1 * a2, a2 * b1 + b2

    _, h = lax.associative_scan(combine, (a, u), axis=1)
    return h


def rglru_mixer(p, conv_w, conv_b, w_r, b_r, w_i, b_i, lam):
    bsz, s, _ = p.shape
    xb, yb = _split(p, LRU_WIDTHS)
    xc = _causal_conv(xb, conv_w) + conv_b
    blocks = xc.reshape(bsz, s, LRU_BLOCKS, LRU_BLOCK)
    r = jax.nn.sigmoid(jnp.einsum("btni,nij->btnj", blocks, w_r).reshape(bsz, s, LRU_WIDTH) + b_r)
    i = jax.nn.sigmoid(jnp.einsum("btni,nij->btnj", blocks, w_i).reshape(bsz, s, LRU_WIDTH) + b_i)
    log_a = (-LRU_C * r * jax.nn.softplus(-lam)).astype(jnp.float32)
    u = (xc * i).astype(jnp.float32) * jnp.sqrt(-jnp.expm1(2.0 * log_a))
    h = _linear_scan(jnp.exp(log_a), u)
    return (h * jax.nn.gelu(yb.astype(jnp.float32))).astype(p.dtype)


def _gated_delta_chunked(q, k, v, g, beta):
    bsz, s, nh, dk = q.shape
    dv = v.shape[-1]
    c = GDN_CHUNK
    q, k, v, g, beta = (_to_chunks(t, c) for t in (q * dk ** -0.5, k, v, g, beta))
    gc = jnp.cumsum(g, axis=-1)
    incl = jnp.tril(jnp.ones((c, c), dtype=bool))
    strict = jnp.tril(jnp.ones((c, c), dtype=bool), k=-1)
    decay = jnp.exp(jnp.where(incl, gc[..., :, None] - gc[..., None, :], -jnp.inf))
    k_beta = k * beta[..., None]
    lmat = jnp.where(strict, jnp.einsum("nbhik,nbhjk->nbhij", k_beta, k) * decay, 0.0)
    tmat = lmat + jnp.eye(c, dtype=jnp.float32)
    rhs = jnp.concatenate([v * beta[..., None], k_beta * jnp.exp(gc)[..., None]], axis=-1)
    sol = lax.linalg.triangular_solve(tmat, rhs, left_side=True, lower=True, unit_diagonal=True)
    u, w = sol[..., :dv], sol[..., dv:]
    attn = jnp.einsum("nbhik,nbhjk->nbhij", q, k) * decay
    q_g = q * jnp.exp(gc)[..., None]
    k_g = k * jnp.exp(gc[..., -1:] - gc)[..., None]
    g_last = jnp.exp(gc[..., -1])

    def step(st, inp):
        u_n, w_n, q_n, k_n, a_n, d_n = inp
        v_new = u_n - jnp.einsum("bhck,bhkv->bhcv", w_n, st)
        o = jnp.einsum("bhck,bhkv->bhcv", q_n, st) + jnp.einsum("bhij,bhjv->bhiv", a_n, v_new)
        st = st * d_n[..., None, None] + jnp.einsum("bhck,bhcv->bhkv", k_n, v_new)
        return st, o

    _, o = lax.scan(step, jnp.zeros((bsz, nh, dk, dv), jnp.float32), (u, w, q_g, k_g, attn, g_last))
    return _from_chunks(o)


def gdn_mixer(p, conv_w, a_log, dt_bias, norm_gain):
    bsz, s, _ = p.shape
    q, k, v, zg, b_logit, a_logit = _split(p, GDN_WIDTHS)
    qkv = jax.nn.silu(_causal_conv(jnp.concatenate([q, k, v], axis=-1), conv_w))
    q, k, v = _split(qkv, (GDN_HEADS * GDN_DK, GDN_HEADS * GDN_DK, GDN_HEADS * GDN_DV))
    q = _l2norm(q.reshape(bsz, s, GDN_HEADS, GDN_DK).astype(jnp.float32))
    k = _l2norm(k.reshape(bsz, s, GDN_HEADS, GDN_DK).astype(jnp.float32))
    v = v.reshape(bsz, s, GDN_HEADS, GDN_DV).astype(jnp.float32)
    beta = jax.nn.sigmoid(b_logit.astype(jnp.float32))
    g = -jnp.exp(a_log) * jax.nn.softplus(a_logit.astype(jnp.float32) + dt_bias)
    o = _gated_delta_chunked(q, k, v, g, beta)
    o = rms_norm(o, norm_gain) * jax.nn.silu(zg.reshape(bsz, s, GDN_HEADS, GDN_DV).astype(jnp.float32))
    return o.reshape(bsz, s, GROUP_WIDTH).astype(p.dtype)


def peer_ffn(h, w_query, sub_keys, expert_u, expert_v):
    bsz, s, d = h.shape
    tok = h.reshape(bsz * s, d)
    n = tok.shape[0]
    q = (tok @ w_query).reshape(n, PEER_HEADS, 2, PEER_QDIM // 2)
    sc = jnp.einsum("nhpd,hpkd->nhpk", q, sub_keys).astype(jnp.float32)
    s_top, i_top = lax.top_k(sc, PEER_TOPK)
    cand_s = (s_top[:, :, 0, :, None] + s_top[:, :, 1, None, :]).reshape(n, PEER_HEADS, PEER_TOPK * PEER_TOPK)
    cand_i = (i_top[:, :, 0, :, None] * PEER_KEYS + i_top[:, :, 1, None, :]).reshape(n, PEER_HEADS, PEER_TOPK * PEER_TOPK)
    best_s, pos = lax.top_k(cand_s, PEER_TOPK)
    idx = jnp.take_along_axis(cand_i, pos, axis=-1)
    gate = jax.nn.softmax(best_s, axis=-1)
    nb = n // PEER_TOKEN_BLOCK

    def block(args):
        xb, ib, gb = args
        u = jnp.take(expert_u, ib, axis=0)
        act = jax.nn.gelu(jnp.einsum("td,thkd->thk", xb, u))
        v = jnp.take(expert_v, ib, axis=0)
        return jnp.einsum("thk,thkd->td", (act * gb).astype(v.dtype), v)

    out = lax.map(block, (tok.reshape(nb, PEER_TOKEN_BLOCK, d),
                          idx.reshape(nb, PEER_TOKEN_BLOCK, PEER_HEADS, PEER_TOPK),
                          gate.reshape(nb, PEER_TOKEN_BLOCK, PEER_HEADS, PEER_TOPK)))
    return out.reshape(bsz, s, d).astype(h.dtype)


def setup_inputs(seed: int = 0) -> dict:
    key = jax.random.key(seed)
    keys = jax.random.split(key, 64)
    counter = [0]
    f32 = jnp.float32
    L = DEPTH
    G = GROUP_WIDTH

    def nk():
        counter[0] += 1
        return keys[counter[0] - 1]

    def normal(shape, scale):
        return scale * jax.random.normal(nk(), shape, f32)

    def unif(shape, lo, hi):
        return jax.random.uniform(nk(), shape, f32, lo, hi)

    def gain(shape):
        return 1.0 + normal(shape, 0.02)

    x = normal((BATCH, SEQ, D_MODEL), 1.0)
    norm_mix = gain((L, D_MODEL))
    w_in = normal((L, D_MODEL, IN_COLS), D_MODEL ** -0.5)
    w_out = normal((L, MIX_WIDTH, D_MODEL), MIX_WIDTH ** -0.5)
    rw_mu = unif((L, RW_COLS), 0.0, 1.0)
    rw_w0 = unif((L, G), -6.5, -1.5)
    rw_w_up = normal((L, RW_W_RANK, G), 0.1)
    rw_a0 = normal((L, G), 0.1)
    rw_a_up = normal((L, RW_A_RANK, G), 0.5 * RW_A_RANK ** -0.5)
    rw_g_up = normal((L, RW_G_RANK, G), RW_G_RANK ** -0.5)
    rw_k_k = 0.85 + normal((L, G), 0.02)
    rw_k_a = 1.0 + normal((L, G), 0.02)
    rw_r_k = normal((L, RW_HEADS, RW_HEAD), 0.1)
    rw_ln_w = gain((L, G))
    rw_ln_b = normal((L, G), 0.02)
    rw_v0 = 1.0 + normal((L - 1, G), 0.02)
    rw_v_down = normal((L - 1, G, RW_V_RANK), G ** -0.5)
    rw_v_up = normal((L - 1, RW_V_RANK, G), 0.5 * RW_V_RANK ** -0.5)
    gla_a_up = normal((L, GLA_GATE_RANK, GLA_HEADS * GLA_DK), 0.5 * GLA_GATE_RANK ** -0.5)
    gla_a_bias = unif((L, GLA_HEADS * GLA_DK), -1.0, 4.0)
    gla_norm = gain((L, GLA_DV))
    lru_conv_w = normal((L, CONV_WIDTH, LRU_WIDTH), 0.5)
    lru_conv_b = normal((L, LRU_WIDTH), 0.02)
    lru_w_r = normal((L, LRU_BLOCKS, LRU_BLOCK, LRU_BLOCK), LRU_BLOCK ** -0.5)
    lru_b_r = normal((L, LRU_WIDTH), 0.02)
    lru_w_i = normal((L, LRU_BLOCKS, LRU_BLOCK, LRU_BLOCK), LRU_BLOCK ** -0.5)
    lru_b_i = normal((L, LRU_WIDTH), 0.02)
    a_c = unif((L, LRU_WIDTH), 0.9, 0.999)
    sig = a_c ** (1.0 / LRU_C)
    lru_lam = jnp.log(sig) - jnp.log1p(-sig)
    gdn_conv_w = normal((L, CONV_WIDTH, GDN_CONV_CH), 0.5)
    gdn_a_log = jnp.log(unif((L, GDN_HEADS), 1.0, 16.0))
    dt = jnp.exp(unif((L, GDN_HEADS), math.log(1e-3), math.log(1e-1)))
    gdn_dt_bias = dt + jnp.log(-jnp.expm1(-dt))
    gdn_norm = gain((L, GDN_DV))
    norm_ffn = gain((L, D_MODEL))
    peer_wq = normal((L, D_MODEL, PEER_HEADS * PEER_QDIM), D_MODEL ** -0.5)
    peer_sub_keys = normal((L, PEER_HEADS, 2, PEER_KEYS, PEER_QDIM // 2), (PEER_QDIM // 2) ** -0.5)
    peer_u = normal((L, PEER_EXPERTS, D_MODEL), D_MODEL ** -0.5)
    peer_v = normal((L, PEER_EXPERTS, D_MODEL), PEER_HEADS ** -0.5)
    norm_final = gain((D_MODEL,))
    return {
        "x": x, "norm_mix": norm_mix, "w_in": w_in, "w_out": w_out,
        "rw_mu": rw_mu, "rw_w0": rw_w0, "rw_w_up": rw_w_up, "rw_a0": rw_a0, "rw_a_up": rw_a_up,
        "rw_g_up": rw_g_up, "rw_k_k": rw_k_k, "rw_k_a": rw_k_a, "rw_r_k": rw_r_k,
        "rw_ln_w": rw_ln_w, "rw_ln_b": rw_ln_b, "rw_v0": rw_v0, "rw_v_down": rw_v_down, "rw_v_up": rw_v_up,
        "gla_a_up": gla_a_up, "gla_a_bias": gla_a_bias, "gla_norm": gla_norm,
        "lru_conv_w": lru_conv_w, "lru_conv_b": lru_conv_b, "lru_w_r": lru_w_r, "lru_b_r": lru_b_r,
        "lru_w_i": lru_w_i, "lru_b_i": lru_b_i, "lru_lam": lru_lam,
        "gdn_conv_w": gdn_conv_w, "gdn_a_log": gdn_a_log, "gdn_dt_bias": gdn_dt_bias, "gdn_norm": gdn_norm,
        "norm_ffn": norm_ffn, "peer_wq": peer_wq, "peer_sub_keys": peer_sub_keys,
        "peer_u": peer_u, "peer_v": peer_v, "norm_final": norm_final,
    }


def reference(x, norm_mix, w_in, w_out,
              rw_mu, rw_w0, rw_w_up, rw_a0, rw_a_up, rw_g_up, rw_k_k, rw_k_a, rw_r_k,
              rw_ln_w, rw_ln_b, rw_v0, rw_v_down, rw_v_up,
              gla_a_up, gla_a_bias, gla_norm,
              lru_conv_w, lru_conv_b, lru_w_r, lru_b_r, lru_w_i, lru_b_i, lru_lam,
              gdn_conv_w, gdn_a_log, gdn_dt_bias, gdn_norm,
              norm_ffn, peer_wq, peer_sub_keys, peer_u, peer_v, norm_final):
    v_first = None
    for l in range(DEPTH):
        h = rms_norm(x, norm_mix[l])
        p = h @ w_in[l]
        pa, pb, pc, pd = _split(p, (RW_COLS, GLA_COLS, LRU_COLS, GDN_COLS))
        v_res = None if l == 0 else (rw_v0[l - 1], rw_v_down[l - 1], rw_v_up[l - 1])
        oa, v_first = rwkv7_mixer(pa, v_first, rw_mu[l], rw_w0[l], rw_w_up[l], rw_a0[l], rw_a_up[l],
                                  rw_g_up[l], rw_k_k[l], rw_k_a[l], rw_r_k[l], rw_ln_w[l], rw_ln_b[l], v_res)
        ob = gla_mixer(pb, gla_a_up[l], gla_a_bias[l], gla_norm[l])
        oc = rglru_mixer(pc, lru_conv_w[l], lru_conv_b[l], lru_w_r[l], lru_b_r[l],
                         lru_w_i[l], lru_b_i[l], lru_lam[l])
        od = gdn_mixer(pd, gdn_conv_w[l], gdn_a_log[l], gdn_dt_bias[l], gdn_norm[l])
        x = x + jnp.concatenate([oa, ob, oc, od], axis=-1) @ w_out[l]
        x = x + peer_ffn(rms_norm(x, norm_ffn[l]), peer_wq[l], peer_sub_keys[l], peer_u[l], peer_v[l])
    return rms_norm(x, norm_final)
```

```python
import functools
import math

import jax
import jax.numpy as jnp
import numpy as np
from jax import lax
from jax.experimental import pallas as pl
from jax.experimental.pallas import tpu as pltpu

F32 = jnp.float32
BF16 = jnp.bfloat16

NORM_EPS = 1e-6
GROUP = 512
CHUNK = 64
LANE = 128
SUBLANE = 8
VMEM_LIMIT = 56 * 1024 * 1024

RW_HEADS, RW_HEAD = 8, 64
RW_GN_EPS = 64e-5
RW_PAD_COLS = 3 * GROUP + 3 * LANE
GLA_HEADS, GLA_DK, GLA_DV = 4, 64, 128
GLA_NORMALIZER = 16.0
GLA_PAD_COLS = 2 * GLA_HEADS * GLA_DK + 2 * GROUP + LANE
LRU_C = 8.0
LRU_ROWS = 256
LRU_PAD_COLS = 2 * GROUP
GDN_HEADS, GDN_D = 4, 128
GDN_PAD_COLS = 4 * GROUP + LANE
CONV_W = 4
PEER_HEADS, PEER_KEYS, PEER_TOPK = 8, 128, 16
PEER_HALF = 128
ROUTE_ROWS = 256
PEER_TN = 512
PEER_TE = 512

NT = (((1,), (1,)), ((), ()))
NN = (((1,), (0,)), ((), ()))


def _mm(a, b, dims=NN):
    return lax.dot_general(a.astype(BF16), b.astype(BF16), dims, preferred_element_type=F32)


def _mmhi(a, b, dims=NN):
    return lax.dot_general(a.astype(F32), b.astype(F32), dims, preferred_element_type=F32,
                           precision=lax.Precision.HIGHEST)


def _sigmoid(x):
    return 1.0 / (1.0 + jnp.exp(-x))


def _softplus(x):
    return jnp.maximum(x, 0.0) + jnp.log(1.0 + jnp.exp(-jnp.abs(x)))


def _gelu(x):
    return 0.5 * x * (1.0 + jnp.tanh(math.sqrt(2.0 / math.pi) * (x + 0.044715 * (x * x * x))))


def _silu(x):
    return x * _sigmoid(x)


def _tile_rows(x, n):
    return jnp.concatenate([x] * n, axis=0)


def _full(shape):
    return pl.BlockSpec(shape, lambda *_: (0,) * len(shape))


def _params(*sem):
    return pltpu.CompilerParams(dimension_semantics=sem, vmem_limit_bytes=VMEM_LIMIT)


def _neumann_inverse(n_wide, eye_wide, stack):
    power = n_wide
    inv = eye_wide + n_wide
    for _ in range(int(math.log2(CHUNK)) - 1):
        power = _mmhi(power, stack(power))
        inv = inv + _mmhi(inv, stack(power))
    return inv


def _norm_matmul_kernel(x_ref, g_ref, w_ref, o_ref):
    x = x_ref[...]
    h = x * lax.rsqrt(jnp.mean(x * x, axis=-1, keepdims=True) + NORM_EPS) * g_ref[...]
    o_ref[...] = _mm(h, w_ref[...])


def _norm_matmul(x, gain, w, rows=512):
    n, d = x.shape
    cols = w.shape[1]
    return pl.pallas_call(
        _norm_matmul_kernel,
        grid=(n // rows,),
        in_specs=[pl.BlockSpec((rows, d), lambda i: (i, 0)), _full((1, d)), _full((d, cols))],
        out_specs=pl.BlockSpec((rows, cols), lambda i: (i, 0)),
        out_shape=jax.ShapeDtypeStruct((n, cols), F32),
        compiler_params=_params("parallel"),
        name="norm_in_proj",
    )(x, gain.reshape(1, d), w)


def _rwkv_kernel(has_vres, *refs):
    if has_vres:
        (p_ref, vf_ref, mu_ref, w0_ref, wup_ref, a0_ref, aup_ref, gup_ref, kk_ref, ka_ref, rk_ref, lnw_ref, lnb_ref,
         v0_ref, vdn_ref, vup_ref, hm_ref, tril_ref, strict_ref, incl_ref, eye_ref,
         o_ref, s_ref, last_ref) = refs
    else:
        (p_ref, mu_ref, w0_ref, wup_ref, a0_ref, aup_ref, gup_ref, kk_ref, ka_ref, rk_ref, lnw_ref, lnb_ref,
         hm_ref, tril_ref, strict_ref, incl_ref, eye_ref,
         o_ref, vf_out_ref, s_ref, last_ref) = refs
    c = CHUNK
    g3 = 3 * GROUP

    @pl.when(pl.program_id(1) == 0)
    def _():
        s_ref[...] = jnp.zeros_like(s_ref)
        last_ref[...] = jnp.zeros_like(last_ref)

    p = p_ref[...]
    rows = lax.broadcasted_iota(jnp.int32, (c, 1), 0)
    shifted = jnp.where(rows == 0, last_ref[...], pltpu.roll(p, 1, axis=0))
    last_ref[...] = p[c - 1:c, :]
    z = p + (shifted - p) * mu_ref[...]
    r, k, v = z[:, :GROUP], z[:, GROUP:2 * GROUP], z[:, 2 * GROUP:g3]
    wd, ad, gd = z[:, g3:g3 + LANE], z[:, g3 + LANE:g3 + 2 * LANE], z[:, g3 + 2 * LANE:]
    w_raw = -_softplus(-(w0_ref[...] + _mmhi(jnp.tanh(wd), wup_ref[...]))) - 0.5
    alr = _sigmoid(a0_ref[...] + _mmhi(ad, aup_ref[...]))
    gate = _mmhi(_sigmoid(gd), gup_ref[...])
    if has_vres:
        mix = _sigmoid(v0_ref[...] + _mmhi(_mmhi(v, vdn_ref[...]), vup_ref[...]))
        v = v + (vf_ref[...] - v) * mix
    else:
        vf_out_ref[...] = v
    hm = hm_ref[...]
    kk = k * kk_ref[...]
    kk = kk * lax.rsqrt(jnp.maximum(_mmhi(kk * kk, hm), 1e-24))
    k = k * (1.0 + (alr - 1.0) * ka_ref[...])
    a_vec, b_vec = -kk, kk * alr
    logw = -jnp.exp(w_raw)

    cum = _mmhi(tril_ref[...], logw)
    cum_last = cum[c - 1:c, :]
    e_neg = jnp.exp(-cum)
    e_end = jnp.exp(cum_last - cum)
    a_t = a_vec * jnp.exp(cum - logw)
    r_t = r * jnp.exp(cum)
    b_t, k_t = b_vec * e_neg, k * e_neg
    b_d, k_d = b_vec * e_end, k * e_end

    def stack(x):
        return _tile_rows(x, RW_HEADS) * hm

    strict, incl = strict_ref[...], incl_ref[...]
    lhs = jnp.concatenate([a_t, r_t], axis=0)
    scores = _mm(lhs, jnp.concatenate([stack(b_t), stack(k_t)], axis=0), NT)
    a_ab, a_ak = scores[:c, :GROUP] * strict, scores[:c, GROUP:] * strict
    a_rb, a_rk = scores[c:, :GROUP] * incl, scores[c:, GROUP:] * incl
    inv = _neumann_inverse(a_ab, eye_ref[...], stack)

    state = s_ref[...]
    ars = _mm(lhs, state, NT)
    v_st = stack(v)
    u = _mm(inv, stack(ars[:c] + _mm(a_ak, v_st)))
    y = ars[c:] + _mm(jnp.concatenate([a_rb, a_rk], axis=1), jnp.concatenate([stack(u), v_st], axis=0))
    uv = jnp.concatenate([u, v], axis=0)
    bk = jnp.concatenate([b_d, k_d], axis=0)
    s_ref[...] = state * jnp.exp(cum_last) + _mm(uv.T, bk) * hm

    inv_n = 1.0 / RW_HEAD
    mean = _mmhi(y, hm) * inv_n
    d = y - mean
    var = _mmhi(d * d, hm) * inv_n
    yn = d * lax.rsqrt(var + RW_GN_EPS) * lnw_ref[...] + lnb_ref[...]
    bonus = _mmhi(r * k * rk_ref[...], hm) * v
    o_ref[...] = ((yn + bonus) * gate).astype(o_ref.dtype)


def _rwkv_consts():
    c = CHUNK
    idx = np.arange(GROUP)
    hm = (idx[:, None] // RW_HEAD == idx[None, :] // RW_HEAD).astype(np.float32)
    i = np.arange(c)[:, None]
    j = (np.arange(GROUP) % c)[None, :]
    tril = (np.arange(c)[:, None] >= np.arange(c)[None, :]).astype(np.float32)
    return (jnp.asarray(hm), jnp.asarray(tril), jnp.asarray((i > j).astype(np.float32)),
            jnp.asarray((i >= j).astype(np.float32)), jnp.asarray((i == j).astype(np.float32)))


def _rwkv_mixer(p, v_first, batch, seq, wts, vres):
    n = p.shape[0]
    c = CHUNK
    nc = seq // c
    row = lambda b, i: (b * nc + i, 0)
    consts = _rwkv_consts()
    vec = _full((1, GROUP))
    small = [_full((1, RW_PAD_COLS)), vec, _full((LANE, GROUP)), vec, _full((LANE, GROUP)), _full((LANE, GROUP)),
             vec, vec, vec, vec, vec]
    const_specs = [_full((GROUP, GROUP)), _full((c, c)), _full((c, GROUP)), _full((c, GROUP)), _full((c, GROUP))]
    scratch = [pltpu.VMEM((GROUP, GROUP), F32), pltpu.VMEM((1, RW_PAD_COLS), F32)]
    p_spec = pl.BlockSpec((c, RW_PAD_COLS), row)
    o_spec = pl.BlockSpec((c, GROUP), row)
    if vres is None:
        out, v_first = pl.pallas_call(
            functools.partial(_rwkv_kernel, False),
            grid=(batch, nc),
            in_specs=[p_spec] + small + const_specs,
            out_specs=[o_spec, o_spec],
            out_shape=[jax.ShapeDtypeStruct((n, GROUP), BF16), jax.ShapeDtypeStruct((n, GROUP), F32)],
            scratch_shapes=scratch,
            compiler_params=_params("parallel", "arbitrary"),
            name="rwkv7_first",
        )(p, *wts, *consts)
        return out, v_first
    out = pl.pallas_call(
        functools.partial(_rwkv_kernel, True),
        grid=(batch, nc),
        in_specs=[p_spec, o_spec] + small + [vec, _full((GROUP, LANE)), _full((LANE, GROUP))] + const_specs,
        out_specs=o_spec,
        out_shape=jax.ShapeDtypeStruct((n, GROUP), BF16),
        scratch_shapes=scratch,
        compiler_params=_params("parallel", "arbitrary"),
        name="rwkv7_later",
    )(p, v_first, *wts, *vres, *consts)
    return out, v_first


def _gla_kernel(p_ref, aup_ref, abias_ref, gain_ref, hmk_ref, hmv_ref, hmvk_ref, hm128_ref, tril_ref, incl_ref,
                o_ref, s_ref):
    c = CHUNK
    hk = GLA_HEADS * GLA_DK

    @pl.when(pl.program_id(1) == 0)
    def _():
        s_ref[...] = jnp.zeros_like(s_ref)

    p = p_ref[...]
    q, k = p[:, :hk] * (GLA_DK ** -0.5), p[:, hk:2 * hk]
    v, g = p[:, 2 * hk:2 * hk + GROUP], p[:, 2 * hk + GROUP:2 * hk + 2 * GROUP]
    ad = p[:, 2 * hk + 2 * GROUP:]
    log_a = -_softplus(-(_mm(ad, aup_ref[...]) + abias_ref[...])) * (1.0 / GLA_NORMALIZER)
    b = _mmhi(tril_ref[...], log_a)
    b_ref = b[c // 2 - 1:c // 2, :]
    b_last = b[c - 1:c, :]
    scores = _mm(q * jnp.exp(b - b_ref), _tile_rows(k * jnp.exp(b_ref - b), GLA_HEADS) * hmk_ref[...], NT)
    o = _mm(scores * incl_ref[...], _tile_rows(v, GLA_HEADS) * hmv_ref[...])
    state = s_ref[...]
    o = o + _mm(q * jnp.exp(b), state, NT)
    s_ref[...] = state * jnp.exp(b_last) + _mm(v.T, k * jnp.exp(b_last - b)) * hmvk_ref[...]
    ms = _mmhi(o * o, hm128_ref[...]) * (1.0 / GLA_DV)
    o = o * lax.rsqrt(ms + NORM_EPS) * gain_ref[...]
    o_ref[...] = (o * _silu(g)).astype(o_ref.dtype)


def _gla_consts():
    c = CHUNK
    hk = GLA_HEADS * GLA_DK
    rk, rv = np.arange(hk), np.arange(GROUP)
    hmk = (rk[:, None] // c == rk[None, :] // GLA_DK).astype(np.float32)
    hmv = (rk[:, None] // c == rv[None, :] // GLA_DV).astype(np.float32)
    hmvk = (rv[:, None] // GLA_DV == rk[None, :] // GLA_DK).astype(np.float32)
    hm128 = (rv[:, None] // GLA_DV == rv[None, :] // GLA_DV).astype(np.float32)
    tril = (np.arange(c)[:, None] >= np.arange(c)[None, :]).astype(np.float32)
    incl = (np.arange(c)[:, None] >= (np.arange(GLA_HEADS * c) % c)[None, :]).astype(np.float32)
    return tuple(jnp.asarray(a) for a in (hmk, hmv, hmvk, hm128, tril, incl))


def _gla_mixer(p, batch, seq, wts):
    n = p.shape[0]
    c = CHUNK
    nc = seq // c
    hk = GLA_HEADS * GLA_DK
    row = lambda b, i: (b * nc + i, 0)
    return pl.pallas_call(
        _gla_kernel,
        grid=(batch, nc),
        in_specs=[pl.BlockSpec((c, GLA_PAD_COLS), row), _full((LANE, hk)), _full((1, hk)), _full((1, GROUP)),
                  _full((GLA_HEADS * c, hk)), _full((GLA_HEADS * c, GROUP)), _full((GROUP, hk)),
                  _full((GROUP, GROUP)), _full((c, c)), _full((c, GLA_HEADS * c))],
        out_specs=pl.BlockSpec((c, GROUP), row),
        out_shape=jax.ShapeDtypeStruct((n, GROUP), BF16),
        scratch_shapes=[pltpu.VMEM((GROUP, hk), F32)],
        compiler_params=_params("parallel", "arbitrary"),
        name="gla",
    )(p, *wts, *_gla_consts())


def _causal_conv(x, hist_ref, w_ref):
    rows8 = lax.broadcasted_iota(jnp.int32, (SUBLANE, 1), 0)
    hist = hist_ref[...]
    y = x * w_ref[CONV_W - 1:CONV_W, :]
    for d in range(1, CONV_W):
        xr = pltpu.roll(x, d, axis=0)
        head = jnp.where(rows8 < d, pltpu.roll(hist, d, axis=0), xr[:SUBLANE])
        y = y + jnp.concatenate([head, xr[SUBLANE:]], axis=0) * w_ref[CONV_W - 1 - d:CONV_W - d, :]
    hist_ref[...] = x[x.shape[0] - SUBLANE:, :]
    return y


def _lru_kernel(p_ref, cw_ref, cb_ref, wr_ref, br_ref, wi_ref, bi_ref, lam_ref, o_ref, hist_ref, h_ref):
    t = LRU_ROWS

    @pl.when(pl.program_id(1) == 0)
    def _():
        hist_ref[...] = jnp.zeros_like(hist_ref)
        h_ref[...] = jnp.zeros_like(h_ref)

    p = p_ref[...]
    xc = _causal_conv(p[:, :GROUP], hist_ref, cw_ref) + cb_ref[...]
    r = _sigmoid(_mm(xc, wr_ref[...]) + br_ref[...])
    i = _sigmoid(_mm(xc, wi_ref[...]) + bi_ref[...])
    log_a = -LRU_C * r * _softplus(-lam_ref[...])
    u = xc * i * jnp.sqrt(1.0 - jnp.exp(2.0 * log_a))
    a = jnp.exp(log_a)
    rows = lax.broadcasted_iota(jnp.int32, (t, 1), 0)
    d = 1
    while d < t:
        keep = rows >= d
        a_s = jnp.where(keep, pltpu.roll(a, d, axis=0), 1.0)
        u_s = jnp.where(keep, pltpu.roll(u, d, axis=0), 0.0)
        u = a * u_s + u
        a = a * a_s
        d *= 2
    h = u + a * h_ref[...]
    h_ref[...] = h[t - 1:t, :]
    o_ref[...] = (h * _gelu(p[:, GROUP:])).astype(o_ref.dtype)


def _lru_mixer(p, batch, seq, wts):
    n = p.shape[0]
    t = LRU_ROWS
    nt = seq // t
    row = lambda b, i: (b * nt + i, 0)
    vec = _full((1, GROUP))
    return pl.pallas_call(
        _lru_kernel,
        grid=(batch, nt),
        in_specs=[pl.BlockSpec((t, LRU_PAD_COLS), row), _full((CONV_W, GROUP)), vec, _full((GROUP, GROUP)), vec,
                  _full((GROUP, GROUP)), vec, vec],
        out_specs=pl.BlockSpec((t, GROUP), row),
        out_shape=jax.ShapeDtypeStruct((n, GROUP), BF16),
        scratch_shapes=[pltpu.VMEM((SUBLANE, GROUP), F32), pltpu.VMEM((1, GROUP), F32)],
        compiler_params=_params("parallel", "arbitrary"),
        name="rglru",
    )(p, *wts)


def _gdn_kernel(p_ref, cw_ref, alog_ref, dtb_ref, gain_ref, tril_ref, o_ref, hist_ref, s_ref):
    c = CHUNK
    dh = GDN_D
    g3 = 3 * GROUP

    @pl.when(pl.program_id(1) == 0)
    def _():
        hist_ref[...] = jnp.zeros_like(hist_ref)
        s_ref[...] = jnp.zeros_like(s_ref)

    p = p_ref[...]
    qkv = _silu(_causal_conv(p[:, :g3], hist_ref, cw_ref))
    zg = p[:, g3:g3 + GROUP]
    ba = p[:, g3 + GROUP:]
    beta_all = _sigmoid(ba)
    g_all = -jnp.exp(alog_ref[...]) * _softplus(ba + dtb_ref[...])
    gc_all = _mmhi(tril_ref[...], g_all)
    gc_rows = gc_all.T
    ri = lax.broadcasted_iota(jnp.int32, (c, c), 0)
    ci = lax.broadcasted_iota(jnp.int32, (c, c), 1)
    incl, strict = ri >= ci, ri > ci
    eye = (ri == ci).astype(F32)
    outs = []
    for h in range(GDN_HEADS):
        sl = slice(h * dh, (h + 1) * dh)
        q = qkv[:, sl]
        k = qkv[:, GROUP + h * dh:GROUP + (h + 1) * dh]
        v = qkv[:, 2 * GROUP + h * dh:2 * GROUP + (h + 1) * dh]
        q = q * lax.rsqrt(jnp.sum(q * q, axis=-1, keepdims=True) + 1e-6) * (dh ** -0.5)
        k = k * lax.rsqrt(jnp.sum(k * k, axis=-1, keepdims=True) + 1e-6)
        beta = beta_all[:, h:h + 1]
        gc = gc_all[:, GDN_HEADS + h:GDN_HEADS + h + 1]
        gc_row = gc_rows[GDN_HEADS + h:GDN_HEADS + h + 1, :]
        gc_last = gc[c - 1:c, :]
        decay = jnp.where(incl, jnp.exp(jnp.minimum(gc - gc_row, 0.0)), 0.0)
        kb = k * beta
        lmat = jnp.where(strict, _mm(kb, k, NT) * decay, 0.0)
        tinv = _neumann_inverse(-lmat, eye, lambda x: x)
        e_gc = jnp.exp(gc)
        u = _mmhi(tinv, v * beta)
        w = _mmhi(tinv, kb * e_gc)
        attn = _mm(q, k, NT) * decay
        state = s_ref[h]
        v_new = u - _mm(w, state)
        o = _mm(q * e_gc, state) + _mm(attn, v_new)
        s_ref[h] = state * jnp.exp(gc_last) + _mm((k * jnp.exp(gc_last - gc)).T, v_new)
        o = o * lax.rsqrt(jnp.mean(o * o, axis=-1, keepdims=True) + NORM_EPS) * gain_ref[...]
        outs.append(o * _silu(zg[:, sl]))
    o_ref[...] = jnp.concatenate(outs, axis=1).astype(o_ref.dtype)


def _gdn_mixer(p, batch, seq, wts):
    n = p.shape[0]
    c = CHUNK
    nc = seq // c
    row = lambda b, i: (b * nc + i, 0)
    tril = jnp.asarray((np.arange(c)[:, None] >= np.arange(c)[None, :]).astype(np.float32))
    return pl.pallas_call(
        _gdn_kernel,
        grid=(batch, nc),
        in_specs=[pl.BlockSpec((c, GDN_PAD_COLS), row), _full((CONV_W, 3 * GROUP)), _full((1, LANE)),
                  _full((1, LANE)), _full((1, GDN_D)), _full((c, c))],
        out_specs=pl.BlockSpec((c, GROUP), row),
        out_shape=jax.ShapeDtypeStruct((n, GROUP), BF16),
        scratch_shapes=[pltpu.VMEM((SUBLANE, 3 * GROUP), F32), pltpu.VMEM((GDN_HEADS, GDN_D, GDN_D), F32)],
        compiler_params=_params("parallel", "arbitrary"),
        name="gdn",
    )(p, *wts, tril)


def _out_proj_kernel(x_ref, oa_ref, ob_ref, oc_ref, od_ref, w_ref, g_ref, x_out_ref, h_out_ref):
    acc = x_ref[...]
    for j, o_ref in enumerate((oa_ref, ob_ref, oc_ref, od_ref)):
        acc = acc + _mm(o_ref[...], w_ref[j])
    x_out_ref[...] = acc
    h = acc * lax.rsqrt(jnp.mean(acc * acc, axis=-1, keepdims=True) + NORM_EPS) * g_ref[...]
    h_out_ref[...] = h.astype(h_out_ref.dtype)


def _out_proj(x, outs, w_out, gain, rows=512):
    n, d = x.shape
    xs = pl.BlockSpec((rows, d), lambda i: (i, 0))
    os_ = pl.BlockSpec((rows, GROUP), lambda i: (i, 0))
    return pl.pallas_call(
        _out_proj_kernel,
        grid=(n // rows,),
        in_specs=[xs, os_, os_, os_, os_, _full((4, GROUP, d)), _full((1, d))],
        out_specs=[xs, xs],
        out_shape=[jax.ShapeDtypeStruct((n, d), F32), jax.ShapeDtypeStruct((n, d), BF16)],
        compiler_params=_params("parallel"),
        name="out_proj",
    )(x, *outs, w_out, gain.reshape(1, d))


def _top_values(s, count):
    tops = []
    for _ in range(count):
        m = jnp.max(s, axis=0, keepdims=True)
        tops.append(m)
        s = jnp.where(s == m, -jnp.inf, s)
    return jnp.concatenate(tops, axis=0)


def _route_kernel(h_ref, wq_ref, keys_ref, s1_ref, s2_ref, e1_ref, e2_ref, tau_ref, sc_ref):
    q = _mm(h_ref[...], wq_ref[...])
    for j in range(2 * PEER_HEADS):
        sc_ref[j] = _mm(keys_ref[j], q[:, j * PEER_HALF:(j + 1) * PEER_HALF], NT)

    def per_head(h, carry):
        s1, s2 = sc_ref[2 * h], sc_ref[2 * h + 1]
        a = _top_values(s1, PEER_TOPK)
        b = _top_values(s2, PEER_TOPK)
        cand = jnp.concatenate([a[i:i + 1, :] + b for i in range(PEER_TOPK)], axis=0)
        best = _top_values(cand, PEER_TOPK)
        top = best[0:1, :]
        z = jnp.sum(jnp.exp(best - top), axis=0, keepdims=True)
        s1_ref[h] = s1
        s2_ref[h] = s2
        e1_ref[h] = jnp.exp(s1 - a[0:1, :]) / z
        e2_ref[h] = jnp.exp(s2 - b[0:1, :])
        tau_ref[h] = best[PEER_TOPK - 1:PEER_TOPK, :]
        return carry

    lax.fori_loop(0, PEER_HEADS, per_head, 0)


def _peer_route(h2, wq, keys):
    n, d = h2.shape
    t = ROUTE_ROWS
    big = pl.BlockSpec((PEER_HEADS, PEER_KEYS, t), lambda i: (0, 0, i))
    big_shape = jax.ShapeDtypeStruct((PEER_HEADS, PEER_KEYS, n), F32)
    return pl.pallas_call(
        _route_kernel,
        grid=(n // t,),
        in_specs=[pl.BlockSpec((t, d), lambda i: (i, 0)), _full(wq.shape), _full(keys.shape)],
        out_specs=[big, big, big, big, pl.BlockSpec((PEER_HEADS, 1, t), lambda i: (0, 0, i))],
        out_shape=[big_shape, big_shape, big_shape, big_shape, jax.ShapeDtypeStruct((PEER_HEADS, 1, n), F32)],
        scratch_shapes=[pltpu.VMEM((2 * PEER_HEADS, PEER_KEYS, t), F32)],
        compiler_params=_params("parallel"),
        name="peer_route",
    )(h2, wq, keys)


def _peer_kernel(final, h_ref, x_ref, u_ref, vt_ref, s1_ref, s2_ref, e1_ref, e2_ref, tau_ref, g_ref, o_ref, acc_ref):
    j = pl.program_id(1)

    @pl.when(j == 0)
    def _():
        acc_ref[...] = jnp.zeros_like(acc_ref)

    act = _gelu(_mm(u_ref[...], h_ref[...], NT))
    blocks = []
    for i in range(PEER_TE // PEER_KEYS):
        i1 = j * (PEER_TE // PEER_KEYS) + i
        w = jnp.zeros((PEER_KEYS, PEER_TN), F32)
        for h in range(PEER_HEADS):
            s1 = s1_ref[h, pl.ds(i1, 1), :]
            e1 = e1_ref[h, pl.ds(i1, 1), :]
            w = w + jnp.where(s1 + s2_ref[h] >= tau_ref[h], e1 * e2_ref[h], 0.0)
        blocks.append(w)
    gated = act * jnp.concatenate(blocks, axis=0)
    acc_ref[...] += _mm(vt_ref[...], gated)

    @pl.when(j == pl.num_programs(1) - 1)
    def _():
        y = x_ref[...] + acc_ref[...].T
        if final:
            y = y * lax.rsqrt(jnp.mean(y * y, axis=-1, keepdims=True) + NORM_EPS) * g_ref[...]
        o_ref[...] = y


def _peer_experts(h2, x, u, vt, route, gain, final):
    n, d = x.shape
    e = u.shape[0]
    big = pl.BlockSpec((PEER_HEADS, PEER_KEYS, PEER_TN), lambda i, j: (0, 0, i))
    return pl.pallas_call(
        functools.partial(_peer_kernel, final),
        grid=(n // PEER_TN, e // PEER_TE),
        in_specs=[pl.BlockSpec((PEER_TN, d), lambda i, j: (i, 0)), pl.BlockSpec((PEER_TN, d), lambda i, j: (i, 0)),
                  pl.BlockSpec((PEER_TE, d), lambda i, j: (j, 0)), pl.BlockSpec((d, PEER_TE), lambda i, j: (0, j)),
                  big, big, big, big, pl.BlockSpec((PEER_HEADS, 1, PEER_TN), lambda i, j: (0, 0, i)), _full((1, d))],
        out_specs=pl.BlockSpec((PEER_TN, d), lambda i, j: (i, 0)),
        out_shape=jax.ShapeDtypeStruct((n, d), F32),
        scratch_shapes=[pltpu.VMEM((d, PEER_TN), F32)],
        compiler_params=_params("parallel", "arbitrary"),
        name="peer_experts",
    )(h2, x, u, vt, *route, gain.reshape(1, d))


def _pad_cols(w, width):
    return jnp.pad(w, ((0, 0), (0, width - w.shape[1])))


def _pad_rows(w, height):
    return jnp.pad(w, ((0, height - w.shape[0]), (0, 0)))


def _split_cols(w, widths):
    return jnp.split(w, np.cumsum(widths)[:-1].tolist(), axis=1)


def _block_diag(w):
    nb, bi, bo = w.shape
    eye = jnp.eye(nb, dtype=w.dtype)
    return (eye[:, None, :, None] * w[:, :, None, :]).reshape(nb * bi, nb * bo)


def kernel(x, norm_mix, w_in, w_out, rw_mu, rw_w0, rw_w_up, rw_a0, rw_a_up, rw_g_up, rw_k_k, rw_k_a, rw_r_k, rw_ln_w, rw_ln_b, rw_v0, rw_v_down, rw_v_up, gla_a_up, gla_a_bias, gla_norm, lru_conv_w, lru_conv_b, lru_w_r, lru_b_r, lru_w_i, lru_b_i, lru_lam, gdn_conv_w, gdn_a_log, gdn_dt_bias, gdn_norm, norm_ffn, peer_wq, peer_sub_keys, peer_u, peer_v, norm_final):
    batch, seq, d = x.shape
    depth = w_in.shape[0]
    n = batch * seq
    xf = x.reshape(n, d)
    hk = GLA_HEADS * GLA_DK
    rw_widths = (GROUP, GROUP, GROUP, 64, 64, 128)
    gla_widths = (hk, hk, GROUP, GROUP, 16)
    gdn_widths = (GROUP, GROUP, GROUP, GROUP, GDN_HEADS, GDN_HEADS)
    rw_cols, gla_cols, lru_cols = sum(rw_widths), sum(gla_widths), 2 * GROUP
    row = lambda a: a.reshape(1, -1)
    v_first = None
    for l in range(depth):
        wa, wb, wc, wd = _split_cols(w_in[l], (rw_cols, gla_cols, lru_cols, w_in.shape[2] - rw_cols - gla_cols - lru_cols))
        pr = _split_cols(wa, rw_widths)
        wa = jnp.concatenate(pr[:3] + [_pad_cols(t, LANE) for t in pr[3:]], axis=1).astype(BF16)
        mu = _split_cols(row(rw_mu[l]), rw_widths)
        mu = jnp.concatenate(mu[:3] + [_pad_cols(t, LANE) for t in mu[3:]], axis=1)
        pg = _split_cols(wb, gla_widths)
        wb = jnp.concatenate(pg[:4] + [_pad_cols(pg[4], LANE)], axis=1).astype(BF16)
        pdn = _split_cols(wd, gdn_widths)
        wd = jnp.concatenate(pdn[:4] + [_pad_cols(jnp.concatenate(pdn[4:], axis=1), LANE)], axis=1).astype(BF16)

        pa = _norm_matmul(xf, norm_mix[l], wa)
        pb = _norm_matmul(xf, norm_mix[l], wb)
        pc = _norm_matmul(xf, norm_mix[l], wc.astype(BF16))
        pd_ = _norm_matmul(xf, norm_mix[l], wd)

        rw_wts = (mu, row(rw_w0[l]), _pad_rows(rw_w_up[l], LANE), row(rw_a0[l]), _pad_rows(rw_a_up[l], LANE),
                  rw_g_up[l], row(rw_k_k[l]), row(rw_k_a[l]), row(rw_r_k[l]), row(rw_ln_w[l]), row(rw_ln_b[l]))
        vres = None if l == 0 else (row(rw_v0[l - 1]), _pad_cols(rw_v_down[l - 1], LANE), _pad_rows(rw_v_up[l - 1], LANE))
        oa, v_first = _rwkv_mixer(pa, v_first, batch, seq, rw_wts, vres)
        ob = _gla_mixer(pb, batch, seq, (_pad_rows(gla_a_up[l], LANE), row(gla_a_bias[l]),
                                         row(jnp.tile(gla_norm[l], GLA_HEADS))))
        oc = _lru_mixer(pc, batch, seq, (lru_conv_w[l], row(lru_conv_b[l]), _block_diag(lru_w_r[l]).astype(BF16),
                                         row(lru_b_r[l]), _block_diag(lru_w_i[l]).astype(BF16), row(lru_b_i[l]),
                                         row(lru_lam[l])))
        head_lanes = jnp.zeros((1, LANE), F32)
        od = _gdn_mixer(pd_, batch, seq, (gdn_conv_w[l],
                                          head_lanes.at[0, GDN_HEADS:2 * GDN_HEADS].set(gdn_a_log[l]),
                                          head_lanes.at[0, GDN_HEADS:2 * GDN_HEADS].set(gdn_dt_bias[l]),
                                          row(gdn_norm[l])))
        xf, h2 = _out_proj(xf, (oa, ob, oc, od), w_out[l].reshape(4, GROUP, d).astype(BF16), norm_ffn[l])
        route = _peer_route(h2, peer_wq[l].astype(BF16),
                            peer_sub_keys[l].reshape(2 * PEER_HEADS, PEER_KEYS, PEER_HALF).astype(BF16))
        xf = _peer_experts(h2, xf, peer_u[l].astype(BF16), peer_v[l].T.astype(BF16), route, norm_final,
                           final=(l == depth - 1))
    return xf.reshape(batch, seq, d)
```

```python
import functools
import math

import jax
import jax.numpy as jnp
import numpy as np
from jax import lax
from jax.experimental import pallas as pl
from jax.experimental.pallas import tpu as pltpu

F32 = jnp.float32
BF16 = jnp.bfloat16

NORM_EPS = 1e-6
GROUP = 512
CHUNK = 64
LANE = 128
SUBLANE = 8
VMEM_LIMIT = 56 * 1024 * 1024

RW_HEADS, RW_HEAD = 8, 64
RW_GN_EPS = 64e-5
RW_PAD_COLS = 3 * GROUP + 3 * LANE
GLA_HEADS, GLA_DK, GLA_DV = 4, 64, 128
GLA_NORMALIZER = 16.0
GLA_PAD_COLS = 2 * GLA_HEADS * GLA_DK + 2 * GROUP + LANE
LRU_C = 8.0
LRU_ROWS = 256
LRU_PAD_COLS = 2 * GROUP
GDN_HEADS, GDN_D = 4, 128
GDN_PAD_COLS = 4 * GROUP + LANE
CONV_W = 4
PEER_HEADS, PEER_KEYS, PEER_TOPK = 8, 128, 16
PEER_HALF = 128
ROUTE_ROWS = 256
PEER_TN = 512
PEER_TE = 512

NT = (((1,), (1,)), ((), ()))
NN = (((1,), (0,)), ((), ()))


def _mm(a, b, dims=NN):
    return lax.dot_general(a.astype(BF16), b.astype(BF16), dims, preferred_element_type=F32)


def _split2(x):
    hi = x.astype(BF16)
    return hi, (x - hi.astype(F32)).astype(BF16)


def _mm_mask_rhs(a, mask):
    hi, lo = _split2(a)
    n = a.shape[0]
    out = _mm(jnp.concatenate([hi, lo], axis=0), mask)
    return out[:n] + out[n:]


def _mm_mask_lhs(mask, b):
    hi = b.astype(BF16)
    rest = b - hi.astype(F32)
    mid = rest.astype(BF16)
    lo = (rest - mid.astype(F32)).astype(BF16)
    w = b.shape[1]
    out = _mm(mask, jnp.concatenate([hi, mid, lo], axis=1))
    return out[:, :w] + out[:, w:2 * w] + out[:, 2 * w:]


def _mm3(a, b_ref):
    a_hi, a_lo = _split2(a)
    b_hi, b_lo = b_ref[0], b_ref[1]
    return _mm(a_hi, b_hi) + (_mm(a_lo, b_hi) + _mm(a_hi, b_lo))


def _sigmoid(x):
    return 1.0 / (1.0 + jnp.exp(-x))


def _softplus(x):
    return jnp.maximum(x, 0.0) + jnp.log(1.0 + jnp.exp(-jnp.abs(x)))


def _gelu(x):
    return 0.5 * x * (1.0 + jnp.tanh(math.sqrt(2.0 / math.pi) * (x + 0.044715 * (x * x * x))))


_GELU_C1 = math.sqrt(2.0 / math.pi)
_GELU_C3 = 0.044715 * _GELU_C1


def _silu(x):
    return x * _sigmoid(x)


def _tile_rows(x, n):
    return jnp.concatenate([x] * n, axis=0)


def _full(shape):
    return pl.BlockSpec(shape, lambda *_: (0,) * len(shape))


def _seq_spec(batch, rows, cols):
    return pl.BlockSpec((batch, rows, cols), lambda i: (0, i, 0))


def _params(*sem):
    return pltpu.CompilerParams(dimension_semantics=sem, vmem_limit_bytes=VMEM_LIMIT)


def _neumann_inverse(n_wide, eye_wide, stack):
    power = n_wide
    inv = eye_wide + n_wide
    for _ in range(int(math.log2(CHUNK)) - 1):
        power = _mm(power, stack(power))
        yield
        inv = inv + _mm(inv, stack(power))
        yield
    return inv


def _round_robin(chains):
    chains = list(chains)
    while chains:
        alive = []
        for chain in chains:
            try:
                next(chain)
                alive.append(chain)
            except StopIteration:
                pass
        chains = alive


def _norm_matmul_kernel(x_ref, g_ref, w_ref, o_ref):
    x = x_ref[...]
    h = x * lax.rsqrt(jnp.mean(x * x, axis=-1, keepdims=True) + NORM_EPS) * g_ref[...]
    o_ref[...] = _mm(h, w_ref[...])


def _norm_matmul(x, gain, w, rows=512):
    n, d = x.shape
    cols = w.shape[1]
    return pl.pallas_call(
        _norm_matmul_kernel,
        grid=(n // rows,),
        in_specs=[pl.BlockSpec((rows, d), lambda i: (i, 0)), _full((1, d)), _full((d, cols))],
        out_specs=pl.BlockSpec((rows, cols), lambda i: (i, 0)),
        out_shape=jax.ShapeDtypeStruct((n, cols), F32),
        compiler_params=_params("parallel"),
        name="norm_in_proj",
    )(x, gain.reshape(1, d), w)


def _rwkv_kernel(has_vres, *refs):
    p_ref, s_ref, last_ref = refs[0], refs[-2], refs[-1]

    @pl.when(pl.program_id(0) == 0)
    def _():
        s_ref[...] = jnp.zeros_like(s_ref)
        last_ref[...] = jnp.zeros_like(last_ref)

    _round_robin(_rwkv_chunk(has_vres, b, refs) for b in range(p_ref.shape[0]))


def _rwkv_chunk(has_vres, b, refs):
    if has_vres:
        (p_ref, vf_ref, mu_ref, w0_ref, wup_ref, a0_ref, aup_ref, gup_ref, kk_ref, ka_ref, rk_ref, lnw_ref, lnb_ref,
         v0_ref, vdn_ref, vup_ref, hm_ref, hmf_ref, tril_ref, strict_ref, incl_ref, eye_ref,
         o_ref, s_ref, last_ref) = refs
    else:
        (p_ref, mu_ref, w0_ref, wup_ref, a0_ref, aup_ref, gup_ref, kk_ref, ka_ref, rk_ref, lnw_ref, lnb_ref,
         hm_ref, hmf_ref, tril_ref, strict_ref, incl_ref, eye_ref,
         o_ref, vf_out_ref, s_ref, last_ref) = refs
    c = CHUNK
    g3 = 3 * GROUP
    p = p_ref[b]
    rows = lax.broadcasted_iota(jnp.int32, (c, 1), 0)
    shifted = jnp.where(rows == 0, last_ref[b], pltpu.roll(p, 1, axis=0))
    last_ref[b] = p[c - 1:c, :]
    z = p + (shifted - p) * mu_ref[...]
    r, k, v = z[:, :GROUP], z[:, GROUP:2 * GROUP], z[:, 2 * GROUP:g3]
    wd, ad, gd = z[:, g3:g3 + LANE], z[:, g3 + LANE:g3 + 2 * LANE], z[:, g3 + 2 * LANE:]
    w_raw = -_softplus(-(w0_ref[...] + _mm3(jnp.tanh(wd), wup_ref))) - 0.5
    alr = _sigmoid(a0_ref[...] + _mm3(ad, aup_ref))
    gate = _mm3(_sigmoid(gd), gup_ref)
    yield
    if has_vres:
        low = _mm3(v, vdn_ref)
        yield
        mix = _sigmoid(v0_ref[...] + _mm3(low, vup_ref))
        v = v + (vf_ref[b] - v) * mix
    else:
        vf_out_ref[b] = v
    hm = hm_ref[...]
    kk = k * kk_ref[...]
    kk = kk * lax.rsqrt(jnp.maximum(_mm_mask_rhs(kk * kk, hm), 1e-24))
    k = k * (1.0 + (alr - 1.0) * ka_ref[...])
    a_vec, b_vec = -kk, kk * alr
    logw = -jnp.exp(w_raw)

    cum = _mm_mask_lhs(tril_ref[...], logw)
    yield
    cum_last = cum[c - 1:c, :]
    e_neg = jnp.exp(-cum)
    e_end = jnp.exp(cum_last - cum)
    a_t = a_vec * jnp.exp(cum - logw)
    r_t = r * jnp.exp(cum)
    b_t, k_t = b_vec * e_neg, k * e_neg
    b_d, k_d = b_vec * e_end, k * e_end

    def stack(x):
        return _tile_rows(x.astype(BF16), RW_HEADS) * hm

    strict, incl = strict_ref[...], incl_ref[...]
    lhs = jnp.concatenate([a_t, r_t], axis=0)
    scores = _mm(lhs, jnp.concatenate([stack(b_t), stack(k_t)], axis=0), NT)
    yield
    a_ab, a_ak = scores[:c, :GROUP] * strict, scores[:c, GROUP:] * strict
    a_rb, a_rk = scores[c:, :GROUP] * incl, scores[c:, GROUP:] * incl
    inv = yield from _neumann_inverse(a_ab, eye_ref[...], stack)

    state = s_ref[b]
    ars = _mm(lhs, state, NT)
    v_st = stack(v)
    rhs = ars[:c] + _mm(a_ak, v_st)
    yield
    u = _mm(inv, stack(rhs))
    yield
    y = ars[c:] + _mm(jnp.concatenate([a_rb, a_rk], axis=1), jnp.concatenate([stack(u), v_st], axis=0))
    uv = jnp.concatenate([u, v], axis=0)
    bk = jnp.concatenate([b_d, k_d], axis=0)
    s_ref[b] = state * jnp.exp(cum_last) + _mm(uv.T, bk) * hmf_ref[...]
    yield

    inv_n = 1.0 / RW_HEAD
    mean = _mm_mask_rhs(y, hm) * inv_n
    yield
    d = y - mean
    var = _mm_mask_rhs(d * d, hm) * inv_n
    yield
    yn = d * lax.rsqrt(var + RW_GN_EPS) * lnw_ref[...] + lnb_ref[...]
    bonus = _mm_mask_rhs(r * k * rk_ref[...], hm) * v
    o_ref[b] = ((yn + bonus) * gate).astype(o_ref.dtype)


def _rwkv_consts():
    c = CHUNK
    idx = np.arange(GROUP)
    hm = (idx[:, None] // RW_HEAD == idx[None, :] // RW_HEAD).astype(np.float32)
    i = np.arange(c)[:, None]
    j = (np.arange(GROUP) % c)[None, :]
    tril = (np.arange(c)[:, None] >= np.arange(c)[None, :]).astype(np.float32)
    return (jnp.asarray(hm, BF16), jnp.asarray(hm), jnp.asarray(tril, BF16), jnp.asarray((i > j).astype(np.float32)),
            jnp.asarray((i >= j).astype(np.float32)), jnp.asarray((i == j).astype(np.float32)))


def _split_weight(w):
    hi = w.astype(BF16)
    return jnp.stack([hi, (w - hi.astype(F32)).astype(BF16)])


def _rwkv_mixer(p, v_first, batch, seq, wts, vres):
    c = CHUNK
    consts = _rwkv_consts()
    vec = _full((1, GROUP))
    up = _full((2, LANE, GROUP))
    small = [_full((1, RW_PAD_COLS)), vec, up, vec, up, up, vec, vec, vec, vec, vec]
    const_specs = [_full((GROUP, GROUP)), _full((GROUP, GROUP)), _full((c, c)), _full((c, GROUP)), _full((c, GROUP)),
                   _full((c, GROUP))]
    scratch = [pltpu.VMEM((batch, GROUP, GROUP), F32), pltpu.VMEM((batch, 1, RW_PAD_COLS), F32)]
    p_spec = _seq_spec(batch, c, RW_PAD_COLS)
    o_spec = _seq_spec(batch, c, GROUP)
    if vres is None:
        out, v_first = pl.pallas_call(
            functools.partial(_rwkv_kernel, False),
            grid=(seq // c,),
            in_specs=[p_spec] + small + const_specs,
            out_specs=[o_spec, o_spec],
            out_shape=[jax.ShapeDtypeStruct((batch, seq, GROUP), BF16), jax.ShapeDtypeStruct((batch, seq, GROUP), F32)],
            scratch_shapes=scratch,
            compiler_params=_params("arbitrary"),
            name="rwkv7_first",
        )(p, *wts, *consts)
        return out, v_first
    out = pl.pallas_call(
        functools.partial(_rwkv_kernel, True),
        grid=(seq // c,),
        in_specs=[p_spec, o_spec] + small + [vec, _full((2, GROUP, LANE)), up] + const_specs,
        out_specs=o_spec,
        out_shape=jax.ShapeDtypeStruct((batch, seq, GROUP), BF16),
        scratch_shapes=scratch,
        compiler_params=_params("arbitrary"),
        name="rwkv7_later",
    )(p, v_first, *wts, *vres, *consts)
    return out, v_first


def _gla_kernel(p_ref, aup_ref, abias_ref, gain_ref, hmk_ref, hmv_ref, hmvk_ref, hm128_ref, tril_ref, incl_ref,
                o_ref, s_ref):
    @pl.when(pl.program_id(0) == 0)
    def _():
        s_ref[...] = jnp.zeros_like(s_ref)

    _round_robin(_gla_chunk(b, p_ref, aup_ref, abias_ref, gain_ref, hmk_ref, hmv_ref, hmvk_ref, hm128_ref, tril_ref,
                            incl_ref, o_ref, s_ref) for b in range(p_ref.shape[0]))


def _gla_chunk(seq_id, p_ref, aup_ref, abias_ref, gain_ref, hmk_ref, hmv_ref, hmvk_ref, hm128_ref, tril_ref, incl_ref,
               o_ref, s_ref):
    c = CHUNK
    hk = GLA_HEADS * GLA_DK
    p = p_ref[seq_id]
    q, k = p[:, :hk] * (GLA_DK ** -0.5), p[:, hk:2 * hk]
    v, g = p[:, 2 * hk:2 * hk + GROUP], p[:, 2 * hk + GROUP:2 * hk + 2 * GROUP]
    ad = p[:, 2 * hk + 2 * GROUP:]
    log_a = -_softplus(-(_mm(ad, aup_ref[...]) + abias_ref[...])) * (1.0 / GLA_NORMALIZER)
    yield
    b = _mm_mask_lhs(tril_ref[...], log_a)
    yield
    b_ref = b[c // 2 - 1:c // 2, :]
    b_last = b[c - 1:c, :]
    k_near = (k * jnp.exp(b_ref - b)).astype(BF16)
    scores = _mm(q * jnp.exp(b - b_ref), _tile_rows(k_near, GLA_HEADS) * hmk_ref[...], NT)
    yield
    o = _mm(scores * incl_ref[...], _tile_rows(v.astype(BF16), GLA_HEADS) * hmv_ref[...])
    state = s_ref[seq_id]
    o = o + _mm(q * jnp.exp(b), state, NT)
    s_ref[seq_id] = state * jnp.exp(b_last) + _mm(v.T, k * jnp.exp(b_last - b)) * hmvk_ref[...]
    yield
    ms = _mm_mask_rhs(o * o, hm128_ref[...]) * (1.0 / GLA_DV)
    yield
    o = o * lax.rsqrt(ms + NORM_EPS) * gain_ref[...]
    o_ref[seq_id] = (o * _silu(g)).astype(o_ref.dtype)


def _gla_consts():
    c = CHUNK
    hk = GLA_HEADS * GLA_DK
    rk, rv = np.arange(hk), np.arange(GROUP)
    hmk = (rk[:, None] // c == rk[None, :] // GLA_DK).astype(np.float32)
    hmv = (rk[:, None] // c == rv[None, :] // GLA_DV).astype(np.float32)
    hmvk = (rv[:, None] // GLA_DV == rk[None, :] // GLA_DK).astype(np.float32)
    hm128 = (rv[:, None] // GLA_DV == rv[None, :] // GLA_DV).astype(np.float32)
    tril = (np.arange(c)[:, None] >= np.arange(c)[None, :]).astype(np.float32)
    incl = (np.arange(c)[:, None] >= (np.arange(GLA_HEADS * c) % c)[None, :]).astype(np.float32)
    return (jnp.asarray(hmk, BF16), jnp.asarray(hmv, BF16), jnp.asarray(hmvk), jnp.asarray(hm128, BF16),
            jnp.asarray(tril, BF16), jnp.asarray(incl))


def _gla_mixer(p, batch, seq, wts):
    c = CHUNK
    hk = GLA_HEADS * GLA_DK
    return pl.pallas_call(
        _gla_kernel,
        grid=(seq // c,),
        in_specs=[_seq_spec(batch, c, GLA_PAD_COLS), _full((LANE, hk)), _full((1, hk)), _full((1, GROUP)),
                  _full((GLA_HEADS * c, hk)), _full((GLA_HEADS * c, GROUP)), _full((GROUP, hk)),
                  _full((GROUP, GROUP)), _full((c, c)), _full((c, GLA_HEADS * c))],
        out_specs=_seq_spec(batch, c, GROUP),
        out_shape=jax.ShapeDtypeStruct((batch, seq, GROUP), BF16),
        scratch_shapes=[pltpu.VMEM((batch, GROUP, hk), F32)],
        compiler_params=_params("arbitrary"),
        name="gla",
    )(p, *wts, *_gla_consts())


def _causal_conv(x, hist_ref, w_ref):
    rows8 = lax.broadcasted_iota(jnp.int32, (SUBLANE, 1), 0)
    hist = hist_ref[...]
    y = x * w_ref[CONV_W - 1:CONV_W, :]
    for d in range(1, CONV_W):
        xr = pltpu.roll(x, d, axis=0)
        head = jnp.where(rows8 < d, pltpu.roll(hist, d, axis=0), xr[:SUBLANE])
        y = y + jnp.concatenate([head, xr[SUBLANE:]], axis=0) * w_ref[CONV_W - 1 - d:CONV_W - d, :]
    hist_ref[...] = x[x.shape[0] - SUBLANE:, :]
    return y


def _lru_kernel(p_ref, cw_ref, cb_ref, wr_ref, br_ref, wi_ref, bi_ref, lam_ref, o_ref, hist_ref, h_ref):
    t = LRU_ROWS

    @pl.when(pl.program_id(1) == 0)
    def _():
        hist_ref[...] = jnp.zeros_like(hist_ref)
        h_ref[...] = jnp.zeros_like(h_ref)

    p = p_ref[...]
    xc = _causal_conv(p[:, :GROUP], hist_ref, cw_ref) + cb_ref[...]
    r = _sigmoid(_mm(xc, wr_ref[...]) + br_ref[...])
    i = _sigmoid(_mm(xc, wi_ref[...]) + bi_ref[...])
    log_a = -LRU_C * r * _softplus(-lam_ref[...])
    u = xc * i * jnp.sqrt(1.0 - jnp.exp(2.0 * log_a))
    a = jnp.exp(log_a)
    rows = lax.broadcasted_iota(jnp.int32, (t, 1), 0)
    d = 1
    while d < t:
        keep = rows >= d
        a_s = jnp.where(keep, pltpu.roll(a, d, axis=0), 1.0)
        u_s = jnp.where(keep, pltpu.roll(u, d, axis=0), 0.0)
        u = a * u_s + u
        a = a * a_s
        d *= 2
    h = u + a * h_ref[...]
    h_ref[...] = h[t - 1:t, :]
    o_ref[...] = (h * _gelu(p[:, GROUP:])).astype(o_ref.dtype)


def _lru_mixer(p, batch, seq, wts):
    n = p.shape[0]
    t = LRU_ROWS
    nt = seq // t
    row = lambda b, i: (b * nt + i, 0)
    vec = _full((1, GROUP))
    return pl.pallas_call(
        _lru_kernel,
        grid=(batch, nt),
        in_specs=[pl.BlockSpec((t, LRU_PAD_COLS), row), _full((CONV_W, GROUP)), vec, _full((GROUP, GROUP)), vec,
                  _full((GROUP, GROUP)), vec, vec],
        out_specs=pl.BlockSpec((t, GROUP), row),
        out_shape=jax.ShapeDtypeStruct((n, GROUP), BF16),
        scratch_shapes=[pltpu.VMEM((SUBLANE, GROUP), F32), pltpu.VMEM((1, GROUP), F32)],
        compiler_params=_params("parallel", "arbitrary"),
        name="rglru",
    )(p, *wts)


def _gdn_kernel(p_ref, cw_ref, alog_ref, dtb_ref, gain_ref, tril_ref, o_ref, hist_ref, s_ref):
    @pl.when(pl.program_id(0) == 0)
    def _():
        hist_ref[...] = jnp.zeros_like(hist_ref)
        s_ref[...] = jnp.zeros_like(s_ref)

    c = CHUNK
    g3 = 3 * GROUP
    ri = lax.broadcasted_iota(jnp.int32, (c, c), 0)
    ci = lax.broadcasted_iota(jnp.int32, (c, c), 1)
    masks = (ri >= ci, ri > ci, (ri == ci).astype(F32))
    chains = []
    for b in range(p_ref.shape[0]):
        p = p_ref[b]
        qkv = _silu(_causal_conv(p[:, :g3], hist_ref.at[b], cw_ref))
        zg = p[:, g3:g3 + GROUP]
        ba = p[:, g3 + GROUP:]
        beta_all = _sigmoid(ba)
        g_all = -jnp.exp(alog_ref[...]) * _softplus(ba + dtb_ref[...])
        gc_all = _mm_mask_lhs(tril_ref[...], g_all)
        gc_rows = gc_all.T
        for h in range(GDN_HEADS):
            chains.append(_gdn_head(h, qkv, zg, beta_all, gc_all, gc_rows, masks, gain_ref, o_ref.at[b], s_ref.at[b]))
    _round_robin(chains)


def _gdn_head(h, qkv, zg, beta_all, gc_all, gc_rows, masks, gain_ref, o_ref, s_ref):
    c = CHUNK
    dh = GDN_D
    incl, strict, eye = masks
    sl = slice(h * dh, (h + 1) * dh)
    q = qkv[:, sl]
    k = qkv[:, GROUP + h * dh:GROUP + (h + 1) * dh]
    v = qkv[:, 2 * GROUP + h * dh:2 * GROUP + (h + 1) * dh]
    q = q * lax.rsqrt(jnp.sum(q * q, axis=-1, keepdims=True) + 1e-6) * (dh ** -0.5)
    k = k * lax.rsqrt(jnp.sum(k * k, axis=-1, keepdims=True) + 1e-6)
    beta = beta_all[:, h:h + 1]
    gc = gc_all[:, GDN_HEADS + h:GDN_HEADS + h + 1]
    gc_row = gc_rows[GDN_HEADS + h:GDN_HEADS + h + 1, :]
    gc_last = gc[c - 1:c, :]
    decay = jnp.where(incl, jnp.exp(jnp.minimum(gc - gc_row, 0.0)), 0.0)
    kb = k * beta
    lmat = jnp.where(strict, _mm(kb, k, NT) * decay, 0.0)
    attn = _mm(q, k, NT) * decay
    yield
    tinv = yield from _neumann_inverse(-lmat, eye, lambda x: x)
    e_gc = jnp.exp(gc)
    u = _mm(tinv, v * beta)
    w = _mm(tinv, kb * e_gc)
    yield
    state = s_ref[h]
    v_new = u - _mm(w, state)
    yield
    o = _mm(q * e_gc, state) + _mm(attn, v_new)
    s_ref[h] = state * jnp.exp(gc_last) + _mm((k * jnp.exp(gc_last - gc)).T, v_new)
    yield
    o = o * lax.rsqrt(jnp.mean(o * o, axis=-1, keepdims=True) + NORM_EPS) * gain_ref[...]
    o_ref[:, sl] = (o * _silu(zg[:, sl])).astype(o_ref.dtype)


def _gdn_mixer(p, batch, seq, wts):
    c = CHUNK
    tril = jnp.asarray((np.arange(c)[:, None] >= np.arange(c)[None, :]).astype(np.float32), BF16)
    return pl.pallas_call(
        _gdn_kernel,
        grid=(seq // c,),
        in_specs=[_seq_spec(batch, c, GDN_PAD_COLS), _full((CONV_W, 3 * GROUP)), _full((1, LANE)),
                  _full((1, LANE)), _full((1, GDN_D)), _full((c, c))],
        out_specs=_seq_spec(batch, c, GROUP),
        out_shape=jax.ShapeDtypeStruct((batch, seq, GROUP), BF16),
        scratch_shapes=[pltpu.VMEM((batch, SUBLANE, 3 * GROUP), F32),
                        pltpu.VMEM((batch, GDN_HEADS, GDN_D, GDN_D), F32)],
        compiler_params=_params("arbitrary"),
        name="gdn",
    )(p, *wts, tril)


def _out_proj_kernel(x_ref, oa_ref, ob_ref, oc_ref, od_ref, w_ref, g_ref, x_out_ref, h_out_ref):
    acc = x_ref[...]
    for j, o_ref in enumerate((oa_ref, ob_ref, oc_ref, od_ref)):
        acc = acc + _mm(o_ref[...], w_ref[j])
    x_out_ref[...] = acc
    h = acc * lax.rsqrt(jnp.mean(acc * acc, axis=-1, keepdims=True) + NORM_EPS) * g_ref[...]
    h_out_ref[...] = h.astype(h_out_ref.dtype)


def _out_proj(x, outs, w_out, gain, rows=512):
    n, d = x.shape
    xs = pl.BlockSpec((rows, d), lambda i: (i, 0))
    os_ = pl.BlockSpec((rows, GROUP), lambda i: (i, 0))
    return pl.pallas_call(
        _out_proj_kernel,
        grid=(n // rows,),
        in_specs=[xs, os_, os_, os_, os_, _full((4, GROUP, d)), _full((1, d))],
        out_specs=[xs, xs],
        out_shape=[jax.ShapeDtypeStruct((n, d), F32), jax.ShapeDtypeStruct((n, d), BF16)],
        compiler_params=_params("parallel"),
        name="out_proj",
    )(x, *outs, w_out, gain.reshape(1, d))


def _top_values(s, count):
    tops = []
    for _ in range(count):
        m = jnp.max(s, axis=0, keepdims=True)
        tops.append(m)
        s = jnp.where(s == m, -jnp.inf, s)
    return jnp.concatenate(tops, axis=0)


def _route_kernel(h_ref, wq_ref, keys_ref, m_ref, r2_ref, e1_ref, e2_ref, sc_ref):
    q = _mm(h_ref[...], wq_ref[...])
    for j in range(2 * PEER_HEADS):
        sc_ref[j] = _mm(keys_ref[j], q[:, j * PEER_HALF:(j + 1) * PEER_HALF], NT)

    def per_head(h, carry):
        s1, s2 = sc_ref[2 * h], sc_ref[2 * h + 1]
        a = _top_values(s1, PEER_TOPK)
        b = _top_values(s2, PEER_TOPK)
        cand = jnp.concatenate([a[i:i + 1, :] + b for i in range(PEER_TOPK)], axis=0)
        best = _top_values(cand, PEER_TOPK)
        tau = best[PEER_TOPK - 1:PEER_TOPK, :]
        z = jnp.sum(jnp.exp(best - best[0:1, :]), axis=0, keepdims=True)
        count = jnp.zeros(s1.shape, F32)
        rank2 = jnp.zeros(s2.shape, F32)
        for r in range(PEER_TOPK):
            a_r = a[r:r + 1, :]
            reach = jnp.sum(jnp.where(a_r + b >= tau, 1.0, 0.0), axis=0, keepdims=True)
            count = jnp.where(s1 == a_r, reach, count)
            rank2 = rank2 + jnp.where(s2 < b[r:r + 1, :], 1.0, 0.0)
        m_ref[h] = count
        r2_ref[h] = rank2.astype(BF16)
        e1_ref[h] = jnp.exp(s1 - a[0:1, :]) * (0.5 / z)
        e2_ref[h] = jnp.exp(s2 - b[0:1, :]).astype(BF16)
        return carry

    lax.fori_loop(0, PEER_HEADS, per_head, 0)


def _peer_route(h2, wq, keys):
    n, d = h2.shape
    t = ROUTE_ROWS
    big = pl.BlockSpec((PEER_HEADS, PEER_KEYS, t), lambda i: (0, 0, i))
    big_shape = jax.ShapeDtypeStruct((PEER_HEADS, PEER_KEYS, n), F32)
    return pl.pallas_call(
        _route_kernel,
        grid=(n // t,),
        in_specs=[pl.BlockSpec((t, d), lambda i: (i, 0)), _full(wq.shape), _full(keys.shape)],
        out_specs=[big, big, big, big],
        out_shape=[big_shape, jax.ShapeDtypeStruct(big_shape.shape, BF16), big_shape,
                   jax.ShapeDtypeStruct(big_shape.shape, BF16)],
        scratch_shapes=[pltpu.VMEM((2 * PEER_HEADS, PEER_KEYS, t), F32)],
        compiler_params=_params("parallel"),
        name="peer_route",
    )(h2, wq, keys)


def _peer_step(j, act_ref, act_next_ref, h_ref, u_ref, vt_ref, m_ref, r2_ref, e1_ref, e2_ref, acc_ref, gated_refs,
               mrow_ref, e1row_ref):
    blocks_per_step = PEER_TE // PEER_KEYS
    half_cols = PEER_TN // 2
    packed = 2 * SUBLANE
    first_block = jnp.maximum(j - 1, 0) * blocks_per_step

    def gate_group(q, i, s):
        gated_ref = gated_refs[q]
        cols = slice(q * half_cols + s * LANE, q * half_cols + (s + 1) * LANE)
        groups = range(PEER_KEYS // packed)
        w = [jnp.zeros((packed, LANE), BF16) for _ in groups]
        for h in range(PEER_HEADS):
            count = mrow_ref[i * PEER_HEADS + h, :, cols]
            e1 = e1row_ref[i * PEER_HEADS + h, :, cols]
            for g in groups:
                keys = slice(g * packed, (g + 1) * packed)
                w[g] = jnp.where(r2_ref[h, keys, cols] < count, w[g] + e1 * e2_ref[h, keys, cols], w[g])
        for g in groups:
            rows = slice(i * PEER_KEYS + g * packed, i * PEER_KEYS + (g + 1) * packed)
            x = act_ref[rows, cols]
            inner = x * (_GELU_C1 + _GELU_C3 * (x * x))
            gated_ref[rows, s * LANE:(s + 1) * LANE] = (x * (1.0 + jnp.tanh(inner))).astype(BF16) * w[g]

    act_next_ref[...] = _mm(u_ref[...], h_ref[...], NT)
    for i in range(blocks_per_step):
        for h in range(PEER_HEADS):
            for src, dst in ((m_ref, mrow_ref), (e1_ref, e1row_ref)):
                row = src[h, pl.ds(first_block + i, 1), :]
                dst[i * PEER_HEADS + h] = jnp.broadcast_to(row, (packed, PEER_TN)).astype(BF16)
    for q, gated_ref in enumerate(gated_refs):
        for i in range(blocks_per_step):
            for s in range(half_cols // LANE):
                gate_group(q, i, s)
        qcols = slice(q * half_cols, (q + 1) * half_cols)
        acc_ref[:, qcols] += _mm(vt_ref[...], gated_ref[...])


def _peer_kernel(final, h_ref, x_ref, u_ref, vt_ref, m_ref, r2_ref, e1_ref, e2_ref, g_ref, o_ref,
                 acc_ref, even_ref, odd_ref, gated0_ref, gated1_ref, mrow_ref, e1row_ref):
    j = pl.program_id(1)
    last = pl.num_programs(1) - 1
    parity = lax.rem(j, 2)
    step = functools.partial(_peer_step, j, h_ref=h_ref, u_ref=u_ref, vt_ref=vt_ref, m_ref=m_ref,
                             r2_ref=r2_ref, e1_ref=e1_ref, e2_ref=e2_ref, acc_ref=acc_ref,
                             gated_refs=(gated0_ref, gated1_ref), mrow_ref=mrow_ref, e1row_ref=e1row_ref)

    @pl.when(j == 0)
    def _():
        acc_ref[...] = jnp.zeros_like(acc_ref)
        odd_ref[...] = jnp.zeros_like(odd_ref)

    @pl.when(parity == 0)
    def _():
        step(act_ref=odd_ref, act_next_ref=even_ref)

    @pl.when(parity == 1)
    def _():
        step(act_ref=even_ref, act_next_ref=odd_ref)

    @pl.when(j == last)
    def _():
        y = x_ref[...] + acc_ref[...].T
        if final:
            y = y * lax.rsqrt(jnp.mean(y * y, axis=-1, keepdims=True) + NORM_EPS) * g_ref[...]
        o_ref[...] = y


def _peer_experts(h2, x, u, vt, route, gain, final):
    n, d = x.shape
    tiles = u.shape[0] // PEER_TE
    row_tiles = (PEER_TE // PEER_KEYS) * PEER_HEADS
    once = pl.Buffered(1)
    big = pl.BlockSpec((PEER_HEADS, PEER_KEYS, PEER_TN), lambda i, j: (0, 0, i), pipeline_mode=once)
    tok = pl.BlockSpec((PEER_TN, d), lambda i, j: (i, 0), pipeline_mode=once)
    return pl.pallas_call(
        functools.partial(_peer_kernel, final),
        grid=(n // PEER_TN, tiles + 1),
        in_specs=[tok, tok,
                  pl.BlockSpec((PEER_TE, d), lambda i, j: (jnp.minimum(j, tiles - 1), 0)),
                  pl.BlockSpec((d, PEER_TE), lambda i, j: (0, jnp.maximum(j - 1, 0))),
                  big, big, big, big, _full((1, d))],
        out_specs=pl.BlockSpec((PEER_TN, d), lambda i, j: (i, 0)),
        out_shape=jax.ShapeDtypeStruct((n, d), F32),
        scratch_shapes=[pltpu.VMEM((d, PEER_TN), F32), pltpu.VMEM((PEER_TE, PEER_TN), F32),
                        pltpu.VMEM((PEER_TE, PEER_TN), F32), pltpu.VMEM((PEER_TE, PEER_TN // 2), BF16),
                        pltpu.VMEM((PEER_TE, PEER_TN // 2), BF16),
                        pltpu.VMEM((row_tiles, 2 * SUBLANE, PEER_TN), BF16),
                        pltpu.VMEM((row_tiles, 2 * SUBLANE, PEER_TN), BF16)],
        compiler_params=_params("parallel", "arbitrary"),
        name="peer_experts",
    )(h2, x, u, vt, *route, gain.reshape(1, d))


def _pad_cols(w, width):
    return jnp.pad(w, ((0, 0), (0, width - w.shape[1])))


def _pad_rows(w, height):
    return jnp.pad(w, ((0, height - w.shape[0]), (0, 0)))


def _split_cols(w, widths):
    return jnp.split(w, np.cumsum(widths)[:-1].tolist(), axis=1)


def _block_diag(w):
    nb, bi, bo = w.shape
    eye = jnp.eye(nb, dtype=w.dtype)
    return (eye[:, None, :, None] * w[:, :, None, :]).reshape(nb * bi, nb * bo)


def kernel(x, norm_mix, w_in, w_out, rw_mu, rw_w0, rw_w_up, rw_a0, rw_a_up, rw_g_up, rw_k_k, rw_k_a, rw_r_k, rw_ln_w, rw_ln_b, rw_v0, rw_v_down, rw_v_up, gla_a_up, gla_a_bias, gla_norm, lru_conv_w, lru_conv_b, lru_w_r, lru_b_r, lru_w_i, lru_b_i, lru_lam, gdn_conv_w, gdn_a_log, gdn_dt_bias, gdn_norm, norm_ffn, peer_wq, peer_sub_keys, peer_u, peer_v, norm_final):
    batch, seq, d = x.shape
    depth = w_in.shape[0]
    n = batch * seq
    xf = x.reshape(n, d)
    hk = GLA_HEADS * GLA_DK
    rw_widths = (GROUP, GROUP, GROUP, 64, 64, 128)
    gla_widths = (hk, hk, GROUP, GROUP, 16)
    gdn_widths = (GROUP, GROUP, GROUP, GROUP, GDN_HEADS, GDN_HEADS)
    rw_cols, gla_cols, lru_cols = sum(rw_widths), sum(gla_widths), 2 * GROUP
    row = lambda a: a.reshape(1, -1)
    v_first = None
    for l in range(depth):
        wa, wb, wc, wd = _split_cols(w_in[l], (rw_cols, gla_cols, lru_cols, w_in.shape[2] - rw_cols - gla_cols - lru_cols))
        pr = _split_cols(wa, rw_widths)
        wa = jnp.concatenate(pr[:3] + [_pad_cols(t, LANE) for t in pr[3:]], axis=1).astype(BF16)
        mu = _split_cols(row(rw_mu[l]), rw_widths)
        mu = jnp.concatenate(mu[:3] + [_pad_cols(t, LANE) for t in mu[3:]], axis=1)
        pg = _split_cols(wb, gla_widths)
        wb = jnp.concatenate(pg[:4] + [_pad_cols(pg[4], LANE)], axis=1).astype(BF16)
        pdn = _split_cols(wd, gdn_widths)
        wd = jnp.concatenate(pdn[:4] + [_pad_cols(jnp.concatenate(pdn[4:], axis=1), LANE)], axis=1).astype(BF16)

        seqs = lambda t: t.reshape(batch, seq, t.shape[-1])
        flat = lambda t: t.reshape(n, t.shape[-1])
        pa = seqs(_norm_matmul(xf, norm_mix[l], wa))
        pb = seqs(_norm_matmul(xf, norm_mix[l], wb))
        pc = _norm_matmul(xf, norm_mix[l], wc.astype(BF16))
        pd_ = seqs(_norm_matmul(xf, norm_mix[l], wd))

        rw_wts = (mu, row(rw_w0[l]), _split_weight(_pad_rows(rw_w_up[l], LANE)), row(rw_a0[l]),
                  _split_weight(_pad_rows(rw_a_up[l], LANE)), _split_weight(rw_g_up[l]), row(rw_k_k[l]),
                  row(rw_k_a[l]), row(rw_r_k[l]), row(rw_ln_w[l]), row(rw_ln_b[l]))
        vres = None if l == 0 else (row(rw_v0[l - 1]), _split_weight(_pad_cols(rw_v_down[l - 1], LANE)),
                                    _split_weight(_pad_rows(rw_v_up[l - 1], LANE)))
        oa, v_first = _rwkv_mixer(pa, v_first, batch, seq, rw_wts, vres)
        ob = _gla_mixer(pb, batch, seq, (_pad_rows(gla_a_up[l], LANE), row(gla_a_bias[l]),
                                         row(jnp.tile(gla_norm[l], GLA_HEADS))))
        oc = _lru_mixer(pc, batch, seq, (lru_conv_w[l], row(lru_conv_b[l]), _block_diag(lru_w_r[l]).astype(BF16),
                                         row(lru_b_r[l]), _block_diag(lru_w_i[l]).astype(BF16), row(lru_b_i[l]),
                                         row(lru_lam[l])))
        head_lanes = jnp.zeros((1, LANE), F32)
        od = _gdn_mixer(pd_, batch, seq, (gdn_conv_w[l],
                                          head_lanes.at[0, GDN_HEADS:2 * GDN_HEADS].set(gdn_a_log[l]),
                                          head_lanes.at[0, GDN_HEADS:2 * GDN_HEADS].set(gdn_dt_bias[l]),
                                          row(gdn_norm[l])))
        xf, h2 = _out_proj(xf, (flat(oa), flat(ob), oc, flat(od)), w_out[l].reshape(4, GROUP, d).astype(BF16),
                           norm_ffn[l])
        route = _peer_route(h2, peer_wq[l].astype(BF16),
                            peer_sub_keys[l].reshape(2 * PEER_HEADS, PEER_KEYS, PEER_HALF).astype(BF16))
        xf = _peer_experts(h2, xf, peer_u[l].astype(BF16), peer_v[l].T.astype(BF16), route, norm_final,
                           final=(l == depth - 1))
    return xf.reshape(batch, seq, d)
```

```python
import functools
import math

import jax
import jax.numpy as jnp
import numpy as np
from jax import lax
from jax.experimental import pallas as pl
from jax.experimental.pallas import tpu as pltpu

F32 = jnp.float32
BF16 = jnp.bfloat16

NORM_EPS = 1e-6
GROUP = 512
CHUNK = 64
LANE = 128
SUBLANE = 8
VMEM_LIMIT = 56 * 1024 * 1024

RW_HEADS, RW_HEAD = 8, 64
RW_GN_EPS = 64e-5
RW_PAD_COLS = 3 * GROUP + 3 * LANE
GLA_HEADS, GLA_DK, GLA_DV = 4, 64, 128
GLA_NORMALIZER = 16.0
GLA_PAD_COLS = 2 * GLA_HEADS * GLA_DK + 2 * GROUP + LANE
LRU_C = 8.0
LRU_ROWS = 256
LRU_PAD_COLS = 2 * GROUP
GDN_HEADS, GDN_D = 4, 128
GDN_PAD_COLS = 4 * GROUP + LANE
CONV_W = 4
PEER_HEADS, PEER_KEYS, PEER_TOPK = 8, 128, 16
PEER_HALF = 128
ROUTE_ROWS = 512
PEER_TN = 512
PEER_TE = 512

NT = (((1,), (1,)), ((), ()))
NN = (((1,), (0,)), ((), ()))


def _mm(a, b, dims=NN):
    return lax.dot_general(a.astype(BF16), b.astype(BF16), dims, preferred_element_type=F32)


def _split2(x):
    hi = x.astype(BF16)
    return hi, (x - hi.astype(F32)).astype(BF16)


def _mm_mask_rhs(a, mask):
    hi, lo = _split2(a)
    n = a.shape[0]
    out = _mm(jnp.concatenate([hi, lo], axis=0), mask)
    return out[:n] + out[n:]


def _mm_mask_lhs(mask, b):
    hi = b.astype(BF16)
    rest = b - hi.astype(F32)
    mid = rest.astype(BF16)
    lo = (rest - mid.astype(F32)).astype(BF16)
    w = b.shape[1]
    out = _mm(mask, jnp.concatenate([hi, mid, lo], axis=1))
    return out[:, :w] + out[:, w:2 * w] + out[:, 2 * w:]


def _mm3(a, b_ref):
    a_hi, a_lo = _split2(a)
    b_hi, b_lo = b_ref[0], b_ref[1]
    return _mm(a_hi, b_hi) + (_mm(a_lo, b_hi) + _mm(a_hi, b_lo))


def _sigmoid(x):
    return 1.0 / (1.0 + jnp.exp(-x))


def _softplus(x):
    return jnp.maximum(x, 0.0) + jnp.log(1.0 + jnp.exp(-jnp.abs(x)))


def _gelu(x):
    return 0.5 * x * (1.0 + jnp.tanh(math.sqrt(2.0 / math.pi) * (x + 0.044715 * (x * x * x))))


_GELU_C1 = math.sqrt(2.0 / math.pi)
_GELU_C3 = 0.044715 * _GELU_C1


def _silu(x):
    return x * _sigmoid(x)


def _tile_rows(x, n):
    return jnp.concatenate([x] * n, axis=0)


def _full(shape):
    return pl.BlockSpec(shape, lambda *_: (0,) * len(shape))


def _seq_spec(batch, rows, cols):
    return pl.BlockSpec((batch, rows, cols), lambda i: (0, i, 0))


def _params(*sem):
    return pltpu.CompilerParams(dimension_semantics=sem, vmem_limit_bytes=VMEM_LIMIT)


def _neumann_inverse(n_wide, eye_wide, stack):
    power = n_wide
    inv = eye_wide + n_wide
    for _ in range(int(math.log2(CHUNK)) - 1):
        power = _mm(power, stack(power))
        yield
        inv = inv + _mm(inv, stack(power))
        yield
    return inv


def _round_robin(chains):
    chains = list(chains)
    while chains:
        alive = []
        for chain in chains:
            try:
                next(chain)
                alive.append(chain)
            except StopIteration:
                pass
        chains = alive


def _norm_matmul_kernel(x_ref, g_ref, *refs):
    w_refs, o_refs = refs[:len(refs) // 2], refs[len(refs) // 2:]
    x = x_ref[...]
    h = (x * lax.rsqrt(jnp.mean(x * x, axis=-1, keepdims=True) + NORM_EPS) * g_ref[...]).astype(BF16)
    for w_ref, o_ref in zip(w_refs, o_refs):
        o_ref[...] = _mm(h, w_ref[...])


def _norm_matmul(x, gain, weights, rows=256):
    n, d = x.shape
    resident = pl.Buffered(1)
    return pl.pallas_call(
        _norm_matmul_kernel,
        grid=(n // rows,),
        in_specs=[pl.BlockSpec((rows, d), lambda i: (i, 0)), _full((1, d))]
        + [pl.BlockSpec(w.shape, lambda i: (0, 0), pipeline_mode=resident) for w in weights],
        out_specs=[pl.BlockSpec((rows, w.shape[1]), lambda i: (i, 0)) for w in weights],
        out_shape=[jax.ShapeDtypeStruct((n, w.shape[1]), F32) for w in weights],
        compiler_params=_params("parallel"),
        name="norm_in_proj",
    )(x, gain.reshape(1, d), *weights)


def _rwkv_kernel(has_vres, *refs):
    p_ref, s_ref, last_ref = refs[0], refs[-2], refs[-1]

    @pl.when(pl.program_id(0) == 0)
    def _():
        s_ref[...] = jnp.zeros_like(s_ref)
        last_ref[...] = jnp.zeros_like(last_ref)

    _round_robin(_rwkv_chunk(has_vres, b, refs) for b in range(p_ref.shape[0]))


def _rwkv_chunk(has_vres, b, refs):
    if has_vres:
        (p_ref, vf_ref, mu_ref, w0_ref, wup_ref, a0_ref, aup_ref, gup_ref, kk_ref, ka_ref, rk_ref, lnw_ref, lnb_ref,
         v0_ref, vdn_ref, vup_ref, hm_ref, hmf_ref, tril_ref, strict_ref, incl_ref, eye_ref,
         o_ref, s_ref, last_ref) = refs
    else:
        (p_ref, mu_ref, w0_ref, wup_ref, a0_ref, aup_ref, gup_ref, kk_ref, ka_ref, rk_ref, lnw_ref, lnb_ref,
         hm_ref, hmf_ref, tril_ref, strict_ref, incl_ref, eye_ref,
         o_ref, vf_out_ref, s_ref, last_ref) = refs
    c = CHUNK
    g3 = 3 * GROUP
    p = p_ref[b]
    rows = lax.broadcasted_iota(jnp.int32, (c, 1), 0)
    shifted = jnp.where(rows == 0, last_ref[b], pltpu.roll(p, 1, axis=0))
    last_ref[b] = p[c - 1:c, :]
    z = p + (shifted - p) * mu_ref[...]
    r, k, v = z[:, :GROUP], z[:, GROUP:2 * GROUP], z[:, 2 * GROUP:g3]
    wd, ad, gd = z[:, g3:g3 + LANE], z[:, g3 + LANE:g3 + 2 * LANE], z[:, g3 + 2 * LANE:]
    w_raw = -_softplus(-(w0_ref[...] + _mm3(jnp.tanh(wd), wup_ref))) - 0.5
    alr = _sigmoid(a0_ref[...] + _mm3(ad, aup_ref))
    gate = _mm3(_sigmoid(gd), gup_ref)
    yield
    if has_vres:
        low = _mm3(v, vdn_ref)
        yield
        mix = _sigmoid(v0_ref[...] + _mm3(low, vup_ref))
        v = v + (vf_ref[b] - v) * mix
    else:
        vf_out_ref[b] = v
    hm = hm_ref[...]
    kk = k * kk_ref[...]
    kk = kk * lax.rsqrt(jnp.maximum(_mm_mask_rhs(kk * kk, hm), 1e-24))
    k = k * (1.0 + (alr - 1.0) * ka_ref[...])
    a_vec, b_vec = -kk, kk * alr
    logw = -jnp.exp(w_raw)

    cum = _mm_mask_lhs(tril_ref[...], logw)
    yield
    cum_last = cum[c - 1:c, :]
    e_neg = jnp.exp(-cum)
    e_end = jnp.exp(cum_last - cum)
    a_t = a_vec * jnp.exp(cum - logw)
    r_t = r * jnp.exp(cum)
    b_t, k_t = b_vec * e_neg, k * e_neg
    b_d, k_d = b_vec * e_end, k * e_end

    def stack(x):
        return _tile_rows(x.astype(BF16), RW_HEADS) * hm

    strict, incl = strict_ref[...], incl_ref[...]
    lhs = jnp.concatenate([a_t, r_t], axis=0)
    scores = _mm(lhs, jnp.concatenate([stack(b_t), stack(k_t)], axis=0), NT)
    yield
    a_ab, a_ak = scores[:c, :GROUP] * strict, scores[:c, GROUP:] * strict
    a_rb, a_rk = scores[c:, :GROUP] * incl, scores[c:, GROUP:] * incl
    inv = yield from _neumann_inverse(a_ab, eye_ref[...], stack)

    state = s_ref[b]
    ars = _mm(lhs, state, NT)
    v_st = stack(v)
    rhs = ars[:c] + _mm(a_ak, v_st)
    yield
    u = _mm(inv, stack(rhs))
    yield
    y = ars[c:] + _mm(jnp.concatenate([a_rb, a_rk], axis=1), jnp.concatenate([stack(u), v_st], axis=0))
    uv = jnp.concatenate([u, v], axis=0)
    bk = jnp.concatenate([b_d, k_d], axis=0)
    s_ref[b] = state * jnp.exp(cum_last) + _mm(uv.T, bk) * hmf_ref[...]
    yield

    inv_n = 1.0 / RW_HEAD
    mean = _mm_mask_rhs(y, hm) * inv_n
    yield
    d = y - mean
    var = _mm_mask_rhs(d * d, hm) * inv_n
    yield
    yn = d * lax.rsqrt(var + RW_GN_EPS) * lnw_ref[...] + lnb_ref[...]
    bonus = _mm_mask_rhs(r * k * rk_ref[...], hm) * v
    o_ref[b] = ((yn + bonus) * gate).astype(o_ref.dtype)


def _rwkv_consts():
    c = CHUNK
    idx = np.arange(GROUP)
    hm = (idx[:, None] // RW_HEAD == idx[None, :] // RW_HEAD).astype(np.float32)
    i = np.arange(c)[:, None]
    j = (np.arange(GROUP) % c)[None, :]
    tril = (np.arange(c)[:, None] >= np.arange(c)[None, :]).astype(np.float32)
    return (jnp.asarray(hm, BF16), jnp.asarray(hm), jnp.asarray(tril, BF16), jnp.asarray((i > j).astype(np.float32)),
            jnp.asarray((i >= j).astype(np.float32)), jnp.asarray((i == j).astype(np.float32)))


def _split_weight(w):
    hi = w.astype(BF16)
    return jnp.stack([hi, (w - hi.astype(F32)).astype(BF16)])


def _rwkv_mixer(p, v_first, batch, seq, wts, vres):
    c = CHUNK
    consts = _rwkv_consts()
    vec = _full((1, GROUP))
    up = _full((2, LANE, GROUP))
    small = [_full((1, RW_PAD_COLS)), vec, up, vec, up, up, vec, vec, vec, vec, vec]
    const_specs = [_full((GROUP, GROUP)), _full((GROUP, GROUP)), _full((c, c)), _full((c, GROUP)), _full((c, GROUP)),
                   _full((c, GROUP))]
    scratch = [pltpu.VMEM((batch, GROUP, GROUP), F32), pltpu.VMEM((batch, 1, RW_PAD_COLS), F32)]
    p_spec = _seq_spec(batch, c, RW_PAD_COLS)
    o_spec = _seq_spec(batch, c, GROUP)
    if vres is None:
        out, v_first = pl.pallas_call(
            functools.partial(_rwkv_kernel, False),
            grid=(seq // c,),
            in_specs=[p_spec] + small + const_specs,
            out_specs=[o_spec, o_spec],
            out_shape=[jax.ShapeDtypeStruct((batch, seq, GROUP), BF16), jax.ShapeDtypeStruct((batch, seq, GROUP), F32)],
            scratch_shapes=scratch,
            compiler_params=_params("arbitrary"),
            name="rwkv7_first",
        )(p, *wts, *consts)
        return out, v_first
    out = pl.pallas_call(
        functools.partial(_rwkv_kernel, True),
        grid=(seq // c,),
        in_specs=[p_spec, o_spec] + small + [vec, _full((2, GROUP, LANE)), up] + const_specs,
        out_specs=o_spec,
        out_shape=jax.ShapeDtypeStruct((batch, seq, GROUP), BF16),
        scratch_shapes=scratch,
        compiler_params=_params("arbitrary"),
        name="rwkv7_later",
    )(p, v_first, *wts, *vres, *consts)
    return out, v_first


def _gla_kernel(p_ref, aup_ref, abias_ref, gain_ref, hmk_ref, hmv_ref, hmvk_ref, hm128_ref, tril_ref, incl_ref,
                o_ref, s_ref):
    @pl.when(pl.program_id(0) == 0)
    def _():
        s_ref[...] = jnp.zeros_like(s_ref)

    _round_robin(_gla_chunk(b, p_ref, aup_ref, abias_ref, gain_ref, hmk_ref, hmv_ref, hmvk_ref, hm128_ref, tril_ref,
                            incl_ref, o_ref, s_ref) for b in range(p_ref.shape[0]))


def _gla_chunk(seq_id, p_ref, aup_ref, abias_ref, gain_ref, hmk_ref, hmv_ref, hmvk_ref, hm128_ref, tril_ref, incl_ref,
               o_ref, s_ref):
    c = CHUNK
    hk = GLA_HEADS * GLA_DK
    p = p_ref[seq_id]
    q, k = p[:, :hk] * (GLA_DK ** -0.5), p[:, hk:2 * hk]
    v, g = p[:, 2 * hk:2 * hk + GROUP], p[:, 2 * hk + GROUP:2 * hk + 2 * GROUP]
    ad = p[:, 2 * hk + 2 * GROUP:]
    log_a = -_softplus(-(_mm(ad, aup_ref[...]) + abias_ref[...])) * (1.0 / GLA_NORMALIZER)
    yield
    b = _mm_mask_lhs(tril_ref[...], log_a)
    yield
    b_ref = b[c // 2 - 1:c // 2, :]
    b_last = b[c - 1:c, :]
    k_near = (k * jnp.exp(b_ref - b)).astype(BF16)
    scores = _mm(q * jnp.exp(b - b_ref), _tile_rows(k_near, GLA_HEADS) * hmk_ref[...], NT)
    yield
    o = _mm(scores * incl_ref[...], _tile_rows(v.astype(BF16), GLA_HEADS) * hmv_ref[...])
    state = s_ref[seq_id]
    o = o + _mm(q * jnp.exp(b), state, NT)
    s_ref[seq_id] = state * jnp.exp(b_last) + _mm(v.T, k * jnp.exp(b_last - b)) * hmvk_ref[...]
    yield
    ms = _mm_mask_rhs(o * o, hm128_ref[...]) * (1.0 / GLA_DV)
    yield
    o = o * lax.rsqrt(ms + NORM_EPS) * gain_ref[...]
    o_ref[seq_id] = (o * _silu(g)).astype(o_ref.dtype)


def _gla_consts():
    c = CHUNK
    hk = GLA_HEADS * GLA_DK
    rk, rv = np.arange(hk), np.arange(GROUP)
    hmk = (rk[:, None] // c == rk[None, :] // GLA_DK).astype(np.float32)
    hmv = (rk[:, None] // c == rv[None, :] // GLA_DV).astype(np.float32)
    hmvk = (rv[:, None] // GLA_DV == rk[None, :] // GLA_DK).astype(np.float32)
    hm128 = (rv[:, None] // GLA_DV == rv[None, :] // GLA_DV).astype(np.float32)
    tril = (np.arange(c)[:, None] >= np.arange(c)[None, :]).astype(np.float32)
    incl = (np.arange(c)[:, None] >= (np.arange(GLA_HEADS * c) % c)[None, :]).astype(np.float32)
    return (jnp.asarray(hmk, BF16), jnp.asarray(hmv, BF16), jnp.asarray(hmvk), jnp.asarray(hm128, BF16),
            jnp.asarray(tril, BF16), jnp.asarray(incl))


def _gla_mixer(p, batch, seq, wts):
    c = CHUNK
    hk = GLA_HEADS * GLA_DK
    return pl.pallas_call(
        _gla_kernel,
        grid=(seq // c,),
        in_specs=[_seq_spec(batch, c, GLA_PAD_COLS), _full((LANE, hk)), _full((1, hk)), _full((1, GROUP)),
                  _full((GLA_HEADS * c, hk)), _full((GLA_HEADS * c, GROUP)), _full((GROUP, hk)),
                  _full((GROUP, GROUP)), _full((c, c)), _full((c, GLA_HEADS * c))],
        out_specs=_seq_spec(batch, c, GROUP),
        out_shape=jax.ShapeDtypeStruct((batch, seq, GROUP), BF16),
        scratch_shapes=[pltpu.VMEM((batch, GROUP, hk), F32)],
        compiler_params=_params("arbitrary"),
        name="gla",
    )(p, *wts, *_gla_consts())


def _causal_conv(x, hist_ref, w_ref):
    rows8 = lax.broadcasted_iota(jnp.int32, (SUBLANE, 1), 0)
    hist = hist_ref[...]
    y = x * w_ref[CONV_W - 1:CONV_W, :]
    for d in range(1, CONV_W):
        xr = pltpu.roll(x, d, axis=0)
        head = jnp.where(rows8 < d, pltpu.roll(hist, d, axis=0), xr[:SUBLANE])
        y = y + jnp.concatenate([head, xr[SUBLANE:]], axis=0) * w_ref[CONV_W - 1 - d:CONV_W - d, :]
    hist_ref[...] = x[x.shape[0] - SUBLANE:, :]
    return y


def _lru_kernel(p_ref, cw_ref, cb_ref, wr_ref, br_ref, wi_ref, bi_ref, lam_ref, o_ref, hist_ref, h_ref):
    t = LRU_ROWS

    @pl.when(pl.program_id(1) == 0)
    def _():
        hist_ref[...] = jnp.zeros_like(hist_ref)
        h_ref[...] = jnp.zeros_like(h_ref)

    p = p_ref[...]
    xc = _causal_conv(p[:, :GROUP], hist_ref, cw_ref) + cb_ref[...]
    r = _sigmoid(_mm(xc, wr_ref[...]) + br_ref[...])
    i = _sigmoid(_mm(xc, wi_ref[...]) + bi_ref[...])
    log_a = -LRU_C * r * _softplus(-lam_ref[...])
    u = xc * i * jnp.sqrt(1.0 - jnp.exp(2.0 * log_a))
    a = jnp.exp(log_a)
    rows = lax.broadcasted_iota(jnp.int32, (t, 1), 0)
    d = 1
    while d < t:
        keep = rows >= d
        a_s = jnp.where(keep, pltpu.roll(a, d, axis=0), 1.0)
        u_s = jnp.where(keep, pltpu.roll(u, d, axis=0), 0.0)
        u = a * u_s + u
        a = a * a_s
        d *= 2
    h = u + a * h_ref[...]
    h_ref[...] = h[t - 1:t, :]
    o_ref[...] = (h * _gelu(p[:, GROUP:])).astype(o_ref.dtype)


def _lru_mixer(p, batch, seq, wts):
    n = p.shape[0]
    t = LRU_ROWS
    nt = seq // t
    row = lambda b, i: (b * nt + i, 0)
    vec = _full((1, GROUP))
    return pl.pallas_call(
        _lru_kernel,
        grid=(batch, nt),
        in_specs=[pl.BlockSpec((t, LRU_PAD_COLS), row), _full((CONV_W, GROUP)), vec, _full((GROUP, GROUP)), vec,
                  _full((GROUP, GROUP)), vec, vec],
        out_specs=pl.BlockSpec((t, GROUP), row),
        out_shape=jax.ShapeDtypeStruct((n, GROUP), BF16),
        scratch_shapes=[pltpu.VMEM((SUBLANE, GROUP), F32), pltpu.VMEM((1, GROUP), F32)],
        compiler_params=_params("parallel", "arbitrary"),
        name="rglru",
    )(p, *wts)


def _gdn_kernel(p_ref, cw_ref, alog_ref, dtb_ref, gain_ref, tril_ref, o_ref, hist_ref, s_ref):
    @pl.when(pl.program_id(0) == 0)
    def _():
        hist_ref[...] = jnp.zeros_like(hist_ref)
        s_ref[...] = jnp.zeros_like(s_ref)

    c = CHUNK
    g3 = 3 * GROUP
    ri = lax.broadcasted_iota(jnp.int32, (c, c), 0)
    ci = lax.broadcasted_iota(jnp.int32, (c, c), 1)
    masks = (ri >= ci, ri > ci, (ri == ci).astype(F32))
    chains = []
    for b in range(p_ref.shape[0]):
        p = p_ref[b]
        qkv = _silu(_causal_conv(p[:, :g3], hist_ref.at[b], cw_ref))
        zg = p[:, g3:g3 + GROUP]
        ba = p[:, g3 + GROUP:]
        beta_all = _sigmoid(ba)
        g_all = -jnp.exp(alog_ref[...]) * _softplus(ba + dtb_ref[...])
        gc_all = _mm_mask_lhs(tril_ref[...], g_all)
        gc_rows = gc_all.T
        for h in range(GDN_HEADS):
            chains.append(_gdn_head(h, qkv, zg, beta_all, gc_all, gc_rows, masks, gain_ref, o_ref.at[b], s_ref.at[b]))
    _round_robin(chains)


def _gdn_head(h, qkv, zg, beta_all, gc_all, gc_rows, masks, gain_ref, o_ref, s_ref):
    c = CHUNK
    dh = GDN_D
    incl, strict, eye = masks
    sl = slice(h * dh, (h + 1) * dh)
    q = qkv[:, sl]
    k = qkv[:, GROUP + h * dh:GROUP + (h + 1) * dh]
    v = qkv[:, 2 * GROUP + h * dh:2 * GROUP + (h + 1) * dh]
    q = q * lax.rsqrt(jnp.sum(q * q, axis=-1, keepdims=True) + 1e-6) * (dh ** -0.5)
    k = k * lax.rsqrt(jnp.sum(k * k, axis=-1, keepdims=True) + 1e-6)
    beta = beta_all[:, h:h + 1]
    gc = gc_all[:, GDN_HEADS + h:GDN_HEADS + h + 1]
    gc_row = gc_rows[GDN_HEADS + h:GDN_HEADS + h + 1, :]
    gc_last = gc[c - 1:c, :]
    decay = jnp.where(incl, jnp.exp(jnp.minimum(gc - gc_row, 0.0)), 0.0)
    kb = k * beta
    lmat = jnp.where(strict, _mm(kb, k, NT) * decay, 0.0)
    attn = _mm(q, k, NT) * decay
    yield
    tinv = yield from _neumann_inverse(-lmat, eye, lambda x: x)
    e_gc = jnp.exp(gc)
    u = _mm(tinv, v * beta)
    w = _mm(tinv, kb * e_gc)
    yield
    state = s_ref[h]
    v_new = u - _mm(w, state)
    yield
    o = _mm(q * e_gc, state) + _mm(attn, v_new)
    s_ref[h] = state * jnp.exp(gc_last) + _mm((k * jnp.exp(gc_last - gc)).T, v_new)
    yield
    o = o * lax.rsqrt(jnp.mean(o * o, axis=-1, keepdims=True) + NORM_EPS) * gain_ref[...]
    o_ref[:, sl] = (o * _silu(zg[:, sl])).astype(o_ref.dtype)


def _gdn_mixer(p, batch, seq, wts):
    c = CHUNK
    tril = jnp.asarray((np.arange(c)[:, None] >= np.arange(c)[None, :]).astype(np.float32), BF16)
    return pl.pallas_call(
        _gdn_kernel,
        grid=(seq // c,),
        in_specs=[_seq_spec(batch, c, GDN_PAD_COLS), _full((CONV_W, 3 * GROUP)), _full((1, LANE)),
                  _full((1, LANE)), _full((1, GDN_D)), _full((c, c))],
        out_specs=_seq_spec(batch, c, GROUP),
        out_shape=jax.ShapeDtypeStruct((batch, seq, GROUP), BF16),
        scratch_shapes=[pltpu.VMEM((batch, SUBLANE, 3 * GROUP), F32),
                        pltpu.VMEM((batch, GDN_HEADS, GDN_D, GDN_D), F32)],
        compiler_params=_params("arbitrary"),
        name="gdn",
    )(p, *wts, tril)


def _out_proj_kernel(x_ref, oa_ref, ob_ref, oc_ref, od_ref, w_ref, g_ref, x_out_ref, h_out_ref):
    acc = x_ref[...]
    for j, o_ref in enumerate((oa_ref, ob_ref, oc_ref, od_ref)):
        acc = acc + _mm(o_ref[...], w_ref[j])
    x_out_ref[...] = acc
    h = acc * lax.rsqrt(jnp.mean(acc * acc, axis=-1, keepdims=True) + NORM_EPS) * g_ref[...]
    h_out_ref[...] = h.astype(h_out_ref.dtype)


def _out_proj(x, outs, w_out, gain, rows=512):
    n, d = x.shape
    xs = pl.BlockSpec((rows, d), lambda i: (i, 0))
    os_ = pl.BlockSpec((rows, GROUP), lambda i: (i, 0))
    return pl.pallas_call(
        _out_proj_kernel,
        grid=(n // rows,),
        in_specs=[xs, os_, os_, os_, os_, _full((4, GROUP, d)), _full((1, d))],
        out_specs=[xs, xs],
        out_shape=[jax.ShapeDtypeStruct((n, d), F32), jax.ShapeDtypeStruct((n, d), BF16)],
        compiler_params=_params("parallel"),
        name="out_proj",
    )(x, *outs, w_out, gain.reshape(1, d))


def _top_values(s, count, want_rank=False):
    tops = []
    rank = jnp.full(s.shape, float(count), F32) if want_rank else None
    for r in range(count):
        m = jnp.max(s, axis=0, keepdims=True)
        hit = s == m
        tops.append(m)
        if want_rank:
            rank = jnp.where(hit, float(r), rank)
        s = jnp.where(hit, -jnp.inf, s)
    return jnp.concatenate(tops, axis=0), rank


def _pair_candidates(a, b):
    k = PEER_TOPK
    rows8 = lax.broadcasted_iota(jnp.int32, (SUBLANE, 1), 0)
    parts = [a[0:1, :] + b, a[1:2, :] + b[:SUBLANE]]
    for r in range(2, SUBLANE):
        parts.append(jnp.where(rows8 < k // (r + 1), a[r:r + 1, :] + b[:SUBLANE], -jnp.inf))
    parts.append(a[SUBLANE:, :] + b[0:1, :])
    return jnp.concatenate(parts, axis=0)


def _route_kernel(h_ref, wq_ref, keys_ref, m_ref, r2_ref, e1_ref, e2_ref, sc_ref):
    q = _mm(h_ref[...], wq_ref[...])
    for j in range(2 * PEER_HEADS):
        sc_ref[j] = _mm(keys_ref[j], q[:, j * PEER_HALF:(j + 1) * PEER_HALF], NT)

    def per_head(h, carry):
        s1, s2 = sc_ref[2 * h], sc_ref[2 * h + 1]
        a, _ = _top_values(s1, PEER_TOPK)
        b, rank2 = _top_values(s2, PEER_TOPK, want_rank=True)
        best, _ = _top_values(_pair_candidates(a, b), PEER_TOPK)
        tau = best[PEER_TOPK - 1:PEER_TOPK, :]
        z = jnp.sum(jnp.exp(best - best[0:1, :]), axis=0, keepdims=True)
        count = jnp.zeros(s1.shape, F32)
        for r in range(PEER_TOPK):
            reach = jnp.sum(jnp.where(a[r:r + 1, :] + b >= tau, 1.0, 0.0), axis=0, keepdims=True)
            count = jnp.where(s1 == a[r:r + 1, :], reach, count)
        m_ref[h] = count
        r2_ref[h] = rank2.astype(BF16)
        e1_ref[h] = jnp.exp(s1 - a[0:1, :]) * (0.5 / z)
        e2_ref[h] = jnp.exp(s2 - b[0:1, :]).astype(BF16)
        return carry

    lax.fori_loop(0, PEER_HEADS, per_head, 0)


def _peer_route(h2, wq, keys):
    n, d = h2.shape
    t = ROUTE_ROWS
    big = pl.BlockSpec((PEER_HEADS, PEER_KEYS, t), lambda i: (0, 0, i))
    big_shape = jax.ShapeDtypeStruct((PEER_HEADS, PEER_KEYS, n), F32)
    return pl.pallas_call(
        _route_kernel,
        grid=(n // t,),
        in_specs=[pl.BlockSpec((t, d), lambda i: (i, 0)), _full(wq.shape), _full(keys.shape)],
        out_specs=[big, big, big, big],
        out_shape=[big_shape, jax.ShapeDtypeStruct(big_shape.shape, BF16), big_shape,
                   jax.ShapeDtypeStruct(big_shape.shape, BF16)],
        scratch_shapes=[pltpu.VMEM((2 * PEER_HEADS, PEER_KEYS, t), F32)],
        compiler_params=_params("parallel"),
        name="peer_route",
    )(h2, wq, keys)


def _peer_step(j, act_ref, act_next_ref, h_ref, u_ref, vt_ref, m_ref, r2_ref, e1_ref, e2_ref, acc_ref,
               gated_refs, mrow_ref, e1row_ref):
    blocks_per_step = PEER_TE // PEER_KEYS
    half_cols = PEER_TN // 2
    packed = 2 * SUBLANE
    first_block = jnp.maximum(j - 1, 0) * blocks_per_step

    def gate_group(q, i, s):
        gated_ref = gated_refs[q]
        cols = slice(q * half_cols + s * LANE, q * half_cols + (s + 1) * LANE)
        groups = range(PEER_KEYS // packed)
        w = [jnp.zeros((packed, LANE), BF16) for _ in groups]
        for h in range(PEER_HEADS):
            count = mrow_ref[i * PEER_HEADS + h, :, cols]
            e1 = e1row_ref[i * PEER_HEADS + h, :, cols]
            for g in groups:
                keys = slice(g * packed, (g + 1) * packed)
                w[g] = jnp.where(r2_ref[h, keys, cols] < count, w[g] + e1 * e2_ref[h, keys, cols], w[g])
        for g in groups:
            rows = slice(i * PEER_KEYS + g * packed, i * PEER_KEYS + (g + 1) * packed)
            x = act_ref[rows, cols]
            inner = x * (_GELU_C1 + _GELU_C3 * (x * x))
            gated_ref[rows, s * LANE:(s + 1) * LANE] = (x * (1.0 + jnp.tanh(inner))).astype(BF16) * w[g]

    act_next_ref[...] = _mm(u_ref[...], h_ref[...], NT)
    for i in range(blocks_per_step):
        for h in range(PEER_HEADS):
            for src, dst in ((m_ref, mrow_ref), (e1_ref, e1row_ref)):
                row = src[h, pl.ds(first_block + i, 1), :]
                dst[i * PEER_HEADS + h] = jnp.broadcast_to(row, (packed, PEER_TN)).astype(BF16)
    for q, gated_ref in enumerate(gated_refs):
        for i in range(blocks_per_step):
            for s in range(half_cols // LANE):
                gate_group(q, i, s)
        qcols = slice(q * half_cols, (q + 1) * half_cols)
        acc_ref[:, qcols] += _mm(vt_ref[...], gated_ref[...])


def _peer_kernel(final, h_ref, x_ref, u_ref, vt_ref, m_ref, r2_ref, e1_ref, e2_ref, g_ref, o_ref,
                 acc_ref, even_ref, odd_ref, gated0_ref, gated1_ref, mrow_ref, e1row_ref):
    j = pl.program_id(1)
    last = pl.num_programs(1) - 1
    parity = lax.rem(j, 2)
    step = functools.partial(_peer_step, j, h_ref=h_ref, u_ref=u_ref, vt_ref=vt_ref, m_ref=m_ref,
                             r2_ref=r2_ref, e1_ref=e1_ref, e2_ref=e2_ref, acc_ref=acc_ref,
                             gated_refs=(gated0_ref, gated1_ref), mrow_ref=mrow_ref, e1row_ref=e1row_ref)

    @pl.when(j == 0)
    def _():
        acc_ref[...] = jnp.zeros_like(acc_ref)
        odd_ref[...] = jnp.zeros_like(odd_ref)

    @pl.when(parity == 0)
    def _():
        step(act_ref=odd_ref, act_next_ref=even_ref)

    @pl.when(parity == 1)
    def _():
        step(act_ref=even_ref, act_next_ref=odd_ref)

    @pl.when(j == last)
    def _():
        y = x_ref[...] + acc_ref[...].T
        if final:
            y = y * lax.rsqrt(jnp.mean(y * y, axis=-1, keepdims=True) + NORM_EPS) * g_ref[...]
        o_ref[...] = y


def _peer_experts(h2, x, u, vt, route, gain, final):
    n, d = x.shape
    tiles = u.shape[0] // PEER_TE
    row_tiles = (PEER_TE // PEER_KEYS) * PEER_HEADS
    once = pl.Buffered(1)
    big = pl.BlockSpec((PEER_HEADS, PEER_KEYS, PEER_TN), lambda i, j: (0, 0, i), pipeline_mode=once)
    tok = pl.BlockSpec((PEER_TN, d), lambda i, j: (i, 0), pipeline_mode=once)
    return pl.pallas_call(
        functools.partial(_peer_kernel, final),
        grid=(n // PEER_TN, tiles + 1),
        in_specs=[tok, tok,
                  pl.BlockSpec((PEER_TE, d), lambda i, j: (jnp.minimum(j, tiles - 1), 0)),
                  pl.BlockSpec((None, d, PEER_TE), lambda i, j: (jnp.maximum(j - 1, 0), 0, 0)),
                  big, big, big, big, _full((1, d))],
        out_specs=pl.BlockSpec((PEER_TN, d), lambda i, j: (i, 0)),
        out_shape=jax.ShapeDtypeStruct((n, d), F32),
        scratch_shapes=[pltpu.VMEM((d, PEER_TN), F32), pltpu.VMEM((PEER_TE, PEER_TN), F32),
                        pltpu.VMEM((PEER_TE, PEER_TN), F32), pltpu.VMEM((PEER_TE, PEER_TN // 2), BF16),
                        pltpu.VMEM((PEER_TE, PEER_TN // 2), BF16),
                        pltpu.VMEM((row_tiles, 2 * SUBLANE, PEER_TN), BF16),
                        pltpu.VMEM((row_tiles, 2 * SUBLANE, PEER_TN), BF16)],
        compiler_params=_params("parallel", "arbitrary"),
        name="peer_experts",
    )(h2, x, u, vt, *route, gain.reshape(1, d))


def _pad_cols(w, width):
    return jnp.pad(w, ((0, 0), (0, width - w.shape[1])))


def _pad_rows(w, height):
    return jnp.pad(w, ((0, height - w.shape[0]), (0, 0)))


def _split_cols(w, widths):
    return jnp.split(w, np.cumsum(widths)[:-1].tolist(), axis=1)


def _block_diag(w):
    nb, bi, bo = w.shape
    eye = jnp.eye(nb, dtype=w.dtype)
    return (eye[:, None, :, None] * w[:, :, None, :]).reshape(nb * bi, nb * bo)


def kernel(x, norm_mix, w_in, w_out, rw_mu, rw_w0, rw_w_up, rw_a0, rw_a_up, rw_g_up, rw_k_k, rw_k_a, rw_r_k, rw_ln_w, rw_ln_b, rw_v0, rw_v_down, rw_v_up, gla_a_up, gla_a_bias, gla_norm, lru_conv_w, lru_conv_b, lru_w_r, lru_b_r, lru_w_i, lru_b_i, lru_lam, gdn_conv_w, gdn_a_log, gdn_dt_bias, gdn_norm, norm_ffn, peer_wq, peer_sub_keys, peer_u, peer_v, norm_final):
    batch, seq, d = x.shape
    depth = w_in.shape[0]
    n = batch * seq
    xf = x.reshape(n, d)
    hk = GLA_HEADS * GLA_DK
    rw_widths = (GROUP, GROUP, GROUP, 64, 64, 128)
    gla_widths = (hk, hk, GROUP, GROUP, 16)
    gdn_widths = (GROUP, GROUP, GROUP, GROUP, GDN_HEADS, GDN_HEADS)
    rw_cols, gla_cols, lru_cols = sum(rw_widths), sum(gla_widths), 2 * GROUP
    row = lambda a: a.reshape(1, -1)
    v_first = None
    for l in range(depth):
        wa, wb, wc, wd = _split_cols(w_in[l], (rw_cols, gla_cols, lru_cols, w_in.shape[2] - rw_cols - gla_cols - lru_cols))
        pr = _split_cols(wa, rw_widths)
        wa = jnp.concatenate(pr[:3] + [_pad_cols(t, LANE) for t in pr[3:]], axis=1).astype(BF16)
        mu = _split_cols(row(rw_mu[l]), rw_widths)
        mu = jnp.concatenate(mu[:3] + [_pad_cols(t, LANE) for t in mu[3:]], axis=1)
        pg = _split_cols(wb, gla_widths)
        wb = jnp.concatenate(pg[:4] + [_pad_cols(pg[4], LANE)], axis=1).astype(BF16)
        pdn = _split_cols(wd, gdn_widths)
        wd = jnp.concatenate(pdn[:4] + [_pad_cols(jnp.concatenate(pdn[4:], axis=1), LANE)], axis=1).astype(BF16)

        seqs = lambda t: t.reshape(batch, seq, t.shape[-1])
        flat = lambda t: t.reshape(n, t.shape[-1])
        pa, pb, pc, pd_ = _norm_matmul(xf, norm_mix[l], (wa, wb, wc.astype(BF16), wd))
        pa, pb, pd_ = seqs(pa), seqs(pb), seqs(pd_)

        rw_wts = (mu, row(rw_w0[l]), _split_weight(_pad_rows(rw_w_up[l], LANE)), row(rw_a0[l]),
                  _split_weight(_pad_rows(rw_a_up[l], LANE)), _split_weight(rw_g_up[l]), row(rw_k_k[l]),
                  row(rw_k_a[l]), row(rw_r_k[l]), row(rw_ln_w[l]), row(rw_ln_b[l]))
        vres = None if l == 0 else (row(rw_v0[l - 1]), _split_weight(_pad_cols(rw_v_down[l - 1], LANE)),
                                    _split_weight(_pad_rows(rw_v_up[l - 1], LANE)))
        oa, v_first = _rwkv_mixer(pa, v_first, batch, seq, rw_wts, vres)
        ob = _gla_mixer(pb, batch, seq, (_pad_rows(gla_a_up[l], LANE), row(gla_a_bias[l]),
                                         row(jnp.tile(gla_norm[l], GLA_HEADS))))
        oc = _lru_mixer(pc, batch, seq, (lru_conv_w[l], row(lru_conv_b[l]), _block_diag(lru_w_r[l]).astype(BF16),
                                         row(lru_b_r[l]), _block_diag(lru_w_i[l]).astype(BF16), row(lru_b_i[l]),
                                         row(lru_lam[l])))
        head_lanes = jnp.zeros((1, LANE), F32)
        od = _gdn_mixer(pd_, batch, seq, (gdn_conv_w[l],
                                          head_lanes.at[0, GDN_HEADS:2 * GDN_HEADS].set(gdn_a_log[l]),
                                          head_lanes.at[0, GDN_HEADS:2 * GDN_HEADS].set(gdn_dt_bias[l]),
                                          row(gdn_norm[l])))
        xf, h2 = _out_proj(xf, (flat(oa), flat(ob), oc, flat(od)), w_out[l].reshape(4, GROUP, d).astype(BF16),
                           norm_ffn[l])
        route = _peer_route(h2, peer_wq[l].astype(BF16),
                            peer_sub_keys[l].reshape(2 * PEER_HEADS, PEER_KEYS, PEER_HALF).astype(BF16))
        vt = peer_v[l].reshape(-1, PEER_TE, d).transpose(0, 2, 1).astype(BF16)
        xf = _peer_experts(h2, xf, peer_u[l].astype(BF16), vt, route, norm_final, final=(l == depth - 1))
    return xf.reshape(batch, seq, d)
```

```python
import functools
import math

import jax
import jax.numpy as jnp
import numpy as np
from jax import lax
from jax.experimental import pallas as pl
from jax.experimental.pallas import tpu as pltpu

F32 = jnp.float32
BF16 = jnp.bfloat16

NORM_EPS = 1e-6
GROUP = 512
CHUNK = 64
LANE = 128
SUBLANE = 8
VMEM_LIMIT = 56 * 1024 * 1024

RW_HEADS, RW_HEAD = 8, 64
RW_GN_EPS = 64e-5
RW_PAD_COLS = 3 * GROUP + 3 * LANE
GLA_HEADS, GLA_DK, GLA_DV = 4, 64, 128
GLA_NORMALIZER = 16.0
GLA_PAD_COLS = 2 * GLA_HEADS * GLA_DK + 2 * GROUP + LANE
LRU_C = 8.0
LRU_ROWS = 256
LRU_PAD_COLS = 2 * GROUP
GDN_HEADS, GDN_D = 4, 128
GDN_PAD_COLS = 4 * GROUP + LANE
CONV_W = 4
PEER_HEADS, PEER_KEYS, PEER_TOPK = 8, 128, 16
PEER_HALF = 128
ROUTE_ROWS = 512
PEER_TN = 512
PEER_TE = 1024
PEER_SUB = 512

NT = (((1,), (1,)), ((), ()))
NN = (((1,), (0,)), ((), ()))


def _mm(a, b, dims=NN):
    return lax.dot_general(a.astype(BF16), b.astype(BF16), dims, preferred_element_type=F32)


def _split2(x):
    hi = x.astype(BF16)
    return hi, (x - hi.astype(F32)).astype(BF16)


def _mm_mask_rhs(a, mask):
    hi, lo = _split2(a)
    n = a.shape[0]
    out = _mm(jnp.concatenate([hi, lo], axis=0), mask)
    return out[:n] + out[n:]


def _mm_mask_lhs(mask, b):
    hi = b.astype(BF16)
    rest = b - hi.astype(F32)
    mid = rest.astype(BF16)
    lo = (rest - mid.astype(F32)).astype(BF16)
    w = b.shape[1]
    out = _mm(mask, jnp.concatenate([hi, mid, lo], axis=1))
    return out[:, :w] + out[:, w:2 * w] + out[:, 2 * w:]


def _mm3(a, b_ref):
    a_hi, a_lo = _split2(a)
    b_hi, b_lo = b_ref[0], b_ref[1]
    return _mm(a_hi, b_hi) + (_mm(a_lo, b_hi) + _mm(a_hi, b_lo))


def _sigmoid(x):
    return 1.0 / (1.0 + jnp.exp(-x))


def _softplus(x):
    return jnp.maximum(x, 0.0) + jnp.log(1.0 + jnp.exp(-jnp.abs(x)))


def _gelu(x):
    return 0.5 * x * (1.0 + jnp.tanh(math.sqrt(2.0 / math.pi) * (x + 0.044715 * (x * x * x))))


_GELU_C1 = math.sqrt(2.0 / math.pi)
_GELU_C3 = 0.044715 * _GELU_C1


def _silu(x):
    return x * _sigmoid(x)


def _tile_rows(x, n):
    return jnp.concatenate([x] * n, axis=0)


def _full(shape):
    return pl.BlockSpec(shape, lambda *_: (0,) * len(shape))


def _seq_spec(batch, rows, cols):
    return pl.BlockSpec((batch, rows, cols), lambda i: (0, i, 0))


def _params(*sem):
    return pltpu.CompilerParams(dimension_semantics=sem, vmem_limit_bytes=VMEM_LIMIT)


def _neumann_inverse(n_wide, eye_wide, product):
    power = n_wide
    inv = eye_wide + n_wide
    for _ in range(int(math.log2(CHUNK)) - 1):
        power = product(power, power)
        yield
        inv = inv + product(inv, power)
        yield
    return inv


def _round_robin(chains):
    chains = list(chains)
    while chains:
        alive = []
        for chain in chains:
            try:
                next(chain)
                alive.append(chain)
            except StopIteration:
                pass
        chains = alive


def _norm_matmul_kernel(x_ref, g_ref, *refs):
    w_refs, o_refs = refs[:len(refs) // 2], refs[len(refs) // 2:]
    x = x_ref[...]
    h = (x * lax.rsqrt(jnp.mean(x * x, axis=-1, keepdims=True) + NORM_EPS) * g_ref[...]).astype(BF16)
    for w_ref, o_ref in zip(w_refs, o_refs):
        o_ref[...] = _mm(h, w_ref[...])


def _norm_matmul(x, gain, weights, rows=256):
    n, d = x.shape
    resident = pl.Buffered(1)
    return pl.pallas_call(
        _norm_matmul_kernel,
        grid=(n // rows,),
        in_specs=[pl.BlockSpec((rows, d), lambda i: (i, 0)), _full((1, d))]
        + [pl.BlockSpec(w.shape, lambda i: (0, 0), pipeline_mode=resident) for w in weights],
        out_specs=[pl.BlockSpec((rows, w.shape[1]), lambda i: (i, 0)) for w in weights],
        out_shape=[jax.ShapeDtypeStruct((n, w.shape[1]), F32) for w in weights],
        compiler_params=_params("parallel"),
        name="norm_in_proj",
    )(x, gain.reshape(1, d), *weights)


def _rwkv_kernel(has_vres, *refs):
    p_ref, s_ref, last_ref = refs[0], refs[-2], refs[-1]

    @pl.when(pl.program_id(0) == 0)
    def _():
        s_ref[...] = jnp.zeros_like(s_ref)
        last_ref[...] = jnp.zeros_like(last_ref)

    _round_robin(_rwkv_chunk(has_vres, b, refs) for b in range(p_ref.shape[0]))


def _rwkv_chunk(has_vres, b, refs):
    if has_vres:
        (p_ref, vf_ref, mu_ref, w0_ref, wup_ref, a0_ref, aup_ref, gup_ref, kk_ref, ka_ref, rk_ref, lnw_ref, lnb_ref,
         v0_ref, vdn_ref, vup_ref, hm_ref, hmf_ref, tril_ref, strict_ref, incl_ref, eye_ref,
         o_ref, s_ref, last_ref) = refs
    else:
        (p_ref, mu_ref, w0_ref, wup_ref, a0_ref, aup_ref, gup_ref, kk_ref, ka_ref, rk_ref, lnw_ref, lnb_ref,
         hm_ref, hmf_ref, tril_ref, strict_ref, incl_ref, eye_ref,
         o_ref, vf_out_ref, s_ref, last_ref) = refs
    c = CHUNK
    g3 = 3 * GROUP
    p = p_ref[b]
    rows = lax.broadcasted_iota(jnp.int32, (c, 1), 0)
    shifted = jnp.where(rows == 0, last_ref[b], pltpu.roll(p, 1, axis=0))
    last_ref[b] = p[c - 1:c, :]
    z = p + (shifted - p) * mu_ref[...]
    r, k, v = z[:, :GROUP], z[:, GROUP:2 * GROUP], z[:, 2 * GROUP:g3]
    wd, ad, gd = z[:, g3:g3 + LANE], z[:, g3 + LANE:g3 + 2 * LANE], z[:, g3 + 2 * LANE:]
    w_raw = -_softplus(-(w0_ref[...] + _mm3(jnp.tanh(wd), wup_ref))) - 0.5
    alr = _sigmoid(a0_ref[...] + _mm3(ad, aup_ref))
    gate = _mm3(_sigmoid(gd), gup_ref)
    yield
    if has_vres:
        low = _mm3(v, vdn_ref)
        yield
        mix = _sigmoid(v0_ref[...] + _mm3(low, vup_ref))
        v = v + (vf_ref[b] - v) * mix
    else:
        vf_out_ref[b] = v
    hm = hm_ref[...]
    pairs = [slice(i * LANE, (i + 1) * LANE) for i in range(GROUP // LANE)]

    def head_mm(x, y, dims=NN):
        return jnp.concatenate([_mm(x[:, sl], _tile_rows(y[:, sl].astype(BF16), 2) * hm, dims) for sl in pairs],
                               axis=1)

    def head_sum(x):
        return jnp.concatenate([_mm_mask_rhs(x[:, sl], hm) for sl in pairs], axis=1)

    kk = k * kk_ref[...]
    kk = kk * lax.rsqrt(jnp.maximum(head_sum(kk * kk), 1e-24))
    k = k * (1.0 + (alr - 1.0) * ka_ref[...])
    a_vec, b_vec = -kk, kk * alr
    logw = -jnp.exp(w_raw)

    cum = _mm_mask_lhs(tril_ref[...], logw)
    yield
    cum_last = cum[c - 1:c, :]
    e_neg = jnp.exp(-cum)
    e_end = jnp.exp(cum_last - cum)
    a_t = a_vec * jnp.exp(cum - logw)
    r_t = r * jnp.exp(cum)
    b_t, k_t = b_vec * e_neg, k * e_neg
    b_d, k_d = b_vec * e_end, k * e_end

    strict, incl = strict_ref[...], incl_ref[...]
    lhs = jnp.concatenate([a_t, r_t], axis=0)
    scores_b = head_mm(lhs, b_t, NT)
    scores_k = head_mm(lhs, k_t, NT)
    yield
    a_ab, a_ak = scores_b[:c] * strict, scores_k[:c] * strict
    a_rb, a_rk = scores_b[c:] * incl, scores_k[c:] * incl
    inv = yield from _neumann_inverse(a_ab, eye_ref[...], head_mm)

    ars = jnp.concatenate([_mm(lhs[:, sl], s_ref[b, i], NT) for i, sl in enumerate(pairs)], axis=1)
    rhs = ars[:c] + head_mm(a_ak, v)
    yield
    u = head_mm(inv, rhs)
    yield
    y = ars[c:] + head_mm(a_rb, u) + head_mm(a_rk, v)
    uv = jnp.concatenate([u, v], axis=0)
    bk = jnp.concatenate([b_d, k_d], axis=0)
    decay_all = jnp.exp(cum_last)
    for i, sl in enumerate(pairs):
        s_ref[b, i] = s_ref[b, i] * decay_all[:, sl] + _mm(uv[:, sl].T, bk[:, sl]) * hmf_ref[...]
    yield

    inv_n = 1.0 / RW_HEAD
    mean = head_sum(y) * inv_n
    yield
    d = y - mean
    var = head_sum(d * d) * inv_n
    yield
    yn = d * lax.rsqrt(var + RW_GN_EPS) * lnw_ref[...] + lnb_ref[...]
    bonus = head_sum(r * k * rk_ref[...]) * v
    o_ref[b] = ((yn + bonus) * gate).astype(o_ref.dtype)


def _rwkv_consts():
    c = CHUNK
    idx = np.arange(LANE)
    hm = (idx[:, None] // RW_HEAD == idx[None, :] // RW_HEAD).astype(np.float32)
    i = np.arange(c)[:, None]
    j = (np.arange(GROUP) % c)[None, :]
    tril = (np.arange(c)[:, None] >= np.arange(c)[None, :]).astype(np.float32)
    return (jnp.asarray(hm, BF16), jnp.asarray(hm), jnp.asarray(tril, BF16), jnp.asarray((i > j).astype(np.float32)),
            jnp.asarray((i >= j).astype(np.float32)), jnp.asarray((i == j).astype(np.float32)))


def _split_weight(w):
    hi = w.astype(BF16)
    return jnp.stack([hi, (w - hi.astype(F32)).astype(BF16)])


def _rwkv_mixer(p, v_first, batch, seq, wts, vres):
    c = CHUNK
    consts = _rwkv_consts()
    vec = _full((1, GROUP))
    up = _full((2, LANE, GROUP))
    small = [_full((1, RW_PAD_COLS)), vec, up, vec, up, up, vec, vec, vec, vec, vec]
    const_specs = [_full((LANE, LANE)), _full((LANE, LANE)), _full((c, c)), _full((c, GROUP)), _full((c, GROUP)),
                   _full((c, GROUP))]
    scratch = [pltpu.VMEM((batch, GROUP // LANE, LANE, LANE), F32), pltpu.VMEM((batch, 1, RW_PAD_COLS), F32)]
    p_spec = _seq_spec(batch, c, RW_PAD_COLS)
    o_spec = _seq_spec(batch, c, GROUP)
    if vres is None:
        out, v_first = pl.pallas_call(
            functools.partial(_rwkv_kernel, False),
            grid=(seq // c,),
            in_specs=[p_spec] + small + const_specs,
            out_specs=[o_spec, o_spec],
            out_shape=[jax.ShapeDtypeStruct((batch, seq, GROUP), BF16), jax.ShapeDtypeStruct((batch, seq, GROUP), F32)],
            scratch_shapes=scratch,
            compiler_params=_params("arbitrary"),
            name="rwkv7_first",
        )(p, *wts, *consts)
        return out, v_first
    out = pl.pallas_call(
        functools.partial(_rwkv_kernel, True),
        grid=(seq // c,),
        in_specs=[p_spec, o_spec] + small + [vec, _full((2, GROUP, LANE)), up] + const_specs,
        out_specs=o_spec,
        out_shape=jax.ShapeDtypeStruct((batch, seq, GROUP), BF16),
        scratch_shapes=scratch,
        compiler_params=_params("arbitrary"),
        name="rwkv7_later",
    )(p, v_first, *wts, *vres, *consts)
    return out, v_first


def _gla_kernel(p_ref, aup_ref, abias_ref, gain_ref, hmk_ref, hmv_ref, hmvk_ref, hm128_ref, tril_ref, incl_ref,
                o_ref, s_ref):
    @pl.when(pl.program_id(0) == 0)
    def _():
        s_ref[...] = jnp.zeros_like(s_ref)

    _round_robin(_gla_chunk(b, p_ref, aup_ref, abias_ref, gain_ref, hmk_ref, hmv_ref, hmvk_ref, hm128_ref, tril_ref,
                            incl_ref, o_ref, s_ref) for b in range(p_ref.shape[0]))


def _gla_chunk(seq_id, p_ref, aup_ref, abias_ref, gain_ref, hmk_ref, hmv_ref, hmvk_ref, hm128_ref, tril_ref, incl_ref,
               o_ref, s_ref):
    c = CHUNK
    hk = GLA_HEADS * GLA_DK
    p = p_ref[seq_id]
    q, k = p[:, :hk] * (GLA_DK ** -0.5), p[:, hk:2 * hk]
    v, g = p[:, 2 * hk:2 * hk + GROUP], p[:, 2 * hk + GROUP:2 * hk + 2 * GROUP]
    ad = p[:, 2 * hk + 2 * GROUP:]
    log_a = -_softplus(-(_mm(ad, aup_ref[...]) + abias_ref[...])) * (1.0 / GLA_NORMALIZER)
    yield
    b = _mm_mask_lhs(tril_ref[...], log_a)
    yield
    b_ref = b[c // 2 - 1:c // 2, :]
    b_last = b[c - 1:c, :]
    k_near = (k * jnp.exp(b_ref - b)).astype(BF16)
    scores = _mm(q * jnp.exp(b - b_ref), _tile_rows(k_near, GLA_HEADS) * hmk_ref[...], NT)
    yield
    o = _mm(scores * incl_ref[...], _tile_rows(v.astype(BF16), GLA_HEADS) * hmv_ref[...])
    state = s_ref[seq_id]
    o = o + _mm(q * jnp.exp(b), state, NT)
    s_ref[seq_id] = state * jnp.exp(b_last) + _mm(v.T, k * jnp.exp(b_last - b)) * hmvk_ref[...]
    yield
    ms = _mm_mask_rhs(o * o, hm128_ref[...]) * (1.0 / GLA_DV)
    yield
    o = o * lax.rsqrt(ms + NORM_EPS) * gain_ref[...]
    o_ref[seq_id] = (o * _silu(g)).astype(o_ref.dtype)


def _gla_consts():
    c = CHUNK
    hk = GLA_HEADS * GLA_DK
    rk, rv = np.arange(hk), np.arange(GROUP)
    hmk = (rk[:, None] // c == rk[None, :] // GLA_DK).astype(np.float32)
    hmv = (rk[:, None] // c == rv[None, :] // GLA_DV).astype(np.float32)
    hmvk = (rv[:, None] // GLA_DV == rk[None, :] // GLA_DK).astype(np.float32)
    hm128 = (rv[:, None] // GLA_DV == rv[None, :] // GLA_DV).astype(np.float32)
    tril = (np.arange(c)[:, None] >= np.arange(c)[None, :]).astype(np.float32)
    incl = (np.arange(c)[:, None] >= (np.arange(GLA_HEADS * c) % c)[None, :]).astype(np.float32)
    return (jnp.asarray(hmk, BF16), jnp.asarray(hmv, BF16), jnp.asarray(hmvk), jnp.asarray(hm128, BF16),
            jnp.asarray(tril, BF16), jnp.asarray(incl))


def _gla_mixer(p, batch, seq, wts):
    c = CHUNK
    hk = GLA_HEADS * GLA_DK
    return pl.pallas_call(
        _gla_kernel,
        grid=(seq // c,),
        in_specs=[_seq_spec(batch, c, GLA_PAD_COLS), _full((LANE, hk)), _full((1, hk)), _full((1, GROUP)),
                  _full((GLA_HEADS * c, hk)), _full((GLA_HEADS * c, GROUP)), _full((GROUP, hk)),
                  _full((GROUP, GROUP)), _full((c, c)), _full((c, GLA_HEADS * c))],
        out_specs=_seq_spec(batch, c, GROUP),
        out_shape=jax.ShapeDtypeStruct((batch, seq, GROUP), BF16),
        scratch_shapes=[pltpu.VMEM((batch, GROUP, hk), F32)],
        compiler_params=_params("arbitrary"),
        name="gla",
    )(p, *wts, *_gla_consts())


def _causal_conv(x, hist_ref, w_ref):
    rows8 = lax.broadcasted_iota(jnp.int32, (SUBLANE, 1), 0)
    hist = hist_ref[...]
    y = x * w_ref[CONV_W - 1:CONV_W, :]
    for d in range(1, CONV_W):
        xr = pltpu.roll(x, d, axis=0)
        head = jnp.where(rows8 < d, pltpu.roll(hist, d, axis=0), xr[:SUBLANE])
        y = y + jnp.concatenate([head, xr[SUBLANE:]], axis=0) * w_ref[CONV_W - 1 - d:CONV_W - d, :]
    hist_ref[...] = x[x.shape[0] - SUBLANE:, :]
    return y


def _lru_kernel(p_ref, cw_ref, cb_ref, wr_ref, br_ref, wi_ref, bi_ref, lam_ref, o_ref, hist_ref, h_ref):
    t = LRU_ROWS

    @pl.when(pl.program_id(1) == 0)
    def _():
        hist_ref[...] = jnp.zeros_like(hist_ref)
        h_ref[...] = jnp.zeros_like(h_ref)

    p = p_ref[...]
    xc = _causal_conv(p[:, :GROUP], hist_ref, cw_ref) + cb_ref[...]
    r = _sigmoid(_mm(xc, wr_ref[...]) + br_ref[...])
    i = _sigmoid(_mm(xc, wi_ref[...]) + bi_ref[...])
    log_a = -LRU_C * r * _softplus(-lam_ref[...])
    u = xc * i * jnp.sqrt(1.0 - jnp.exp(2.0 * log_a))
    a = jnp.exp(log_a)
    rows = lax.broadcasted_iota(jnp.int32, (t, 1), 0)
    d = 1
    while d < t:
        keep = rows >= d
        a_s = jnp.where(keep, pltpu.roll(a, d, axis=0), 1.0)
        u_s = jnp.where(keep, pltpu.roll(u, d, axis=0), 0.0)
        u = a * u_s + u
        a = a * a_s
        d *= 2
    h = u + a * h_ref[...]
    h_ref[...] = h[t - 1:t, :]
    o_ref[...] = (h * _gelu(p[:, GROUP:])).astype(o_ref.dtype)


def _lru_mixer(p, batch, seq, wts):
    n = p.shape[0]
    t = LRU_ROWS
    nt = seq // t
    row = lambda b, i: (b * nt + i, 0)
    vec = _full((1, GROUP))
    return pl.pallas_call(
        _lru_kernel,
        grid=(batch, nt),
        in_specs=[pl.BlockSpec((t, LRU_PAD_COLS), row), _full((CONV_W, GROUP)), vec, _full((GROUP, GROUP)), vec,
                  _full((GROUP, GROUP)), vec, vec],
        out_specs=pl.BlockSpec((t, GROUP), row),
        out_shape=jax.ShapeDtypeStruct((n, GROUP), BF16),
        scratch_shapes=[pltpu.VMEM((SUBLANE, GROUP), F32), pltpu.VMEM((1, GROUP), F32)],
        compiler_params=_params("parallel", "arbitrary"),
        name="rglru",
    )(p, *wts)


def _gdn_kernel(p_ref, cw_ref, alog_ref, dtb_ref, gain_ref, tril_ref, o_ref, hist_ref, s_ref):
    @pl.when(pl.program_id(0) == 0)
    def _():
        hist_ref[...] = jnp.zeros_like(hist_ref)
        s_ref[...] = jnp.zeros_like(s_ref)

    c = CHUNK
    g3 = 3 * GROUP
    ri = lax.broadcasted_iota(jnp.int32, (c, c), 0)
    ci = lax.broadcasted_iota(jnp.int32, (c, c), 1)
    masks = (ri >= ci, ri > ci, (ri == ci).astype(F32))
    chains = []
    for b in range(p_ref.shape[0]):
        p = p_ref[b]
        qkv = _silu(_causal_conv(p[:, :g3], hist_ref.at[b], cw_ref))
        zg = p[:, g3:g3 + GROUP]
        ba = p[:, g3 + GROUP:]
        beta_all = _sigmoid(ba)
        g_all = -jnp.exp(alog_ref[...]) * _softplus(ba + dtb_ref[...])
        gc_all = _mm_mask_lhs(tril_ref[...], g_all)
        gc_rows = gc_all.T
        for h in range(GDN_HEADS):
            chains.append(_gdn_head(h, qkv, zg, beta_all, gc_all, gc_rows, masks, gain_ref, o_ref.at[b], s_ref.at[b]))
    _round_robin(chains)


def _gdn_head(h, qkv, zg, beta_all, gc_all, gc_rows, masks, gain_ref, o_ref, s_ref):
    c = CHUNK
    dh = GDN_D
    incl, strict, eye = masks
    sl = slice(h * dh, (h + 1) * dh)
    q = qkv[:, sl]
    k = qkv[:, GROUP + h * dh:GROUP + (h + 1) * dh]
    v = qkv[:, 2 * GROUP + h * dh:2 * GROUP + (h + 1) * dh]
    q = q * lax.rsqrt(jnp.sum(q * q, axis=-1, keepdims=True) + 1e-6) * (dh ** -0.5)
    k = k * lax.rsqrt(jnp.sum(k * k, axis=-1, keepdims=True) + 1e-6)
    beta = beta_all[:, h:h + 1]
    gc = gc_all[:, GDN_HEADS + h:GDN_HEADS + h + 1]
    gc_row = gc_rows[GDN_HEADS + h:GDN_HEADS + h + 1, :]
    gc_last = gc[c - 1:c, :]
    decay = jnp.where(incl, jnp.exp(jnp.minimum(gc - gc_row, 0.0)), 0.0)
    kb = k * beta
    lmat = jnp.where(strict, _mm(kb, k, NT) * decay, 0.0)
    attn = _mm(q, k, NT) * decay
    yield
    tinv = yield from _neumann_inverse(-lmat, eye, _mm)
    e_gc = jnp.exp(gc)
    u = _mm(tinv, v * beta)
    w = _mm(tinv, kb * e_gc)
    yield
    state = s_ref[h]
    v_new = u - _mm(w, state)
    yield
    o = _mm(q * e_gc, state) + _mm(attn, v_new)
    s_ref[h] = state * jnp.exp(gc_last) + _mm((k * jnp.exp(gc_last - gc)).T, v_new)
    yield
    o = o * lax.rsqrt(jnp.mean(o * o, axis=-1, keepdims=True) + NORM_EPS) * gain_ref[...]
    o_ref[:, sl] = (o * _silu(zg[:, sl])).astype(o_ref.dtype)


def _gdn_mixer(p, batch, seq, wts):
    c = CHUNK
    tril = jnp.asarray((np.arange(c)[:, None] >= np.arange(c)[None, :]).astype(np.float32), BF16)
    return pl.pallas_call(
        _gdn_kernel,
        grid=(seq // c,),
        in_specs=[_seq_spec(batch, c, GDN_PAD_COLS), _full((CONV_W, 3 * GROUP)), _full((1, LANE)),
                  _full((1, LANE)), _full((1, GDN_D)), _full((c, c))],
        out_specs=_seq_spec(batch, c, GROUP),
        out_shape=jax.ShapeDtypeStruct((batch, seq, GROUP), BF16),
        scratch_shapes=[pltpu.VMEM((batch, SUBLANE, 3 * GROUP), F32),
                        pltpu.VMEM((batch, GDN_HEADS, GDN_D, GDN_D), F32)],
        compiler_params=_params("arbitrary"),
        name="gdn",
    )(p, *wts, tril)


def _out_proj_kernel(x_ref, oa_ref, ob_ref, oc_ref, od_ref, w_ref, g_ref, x_out_ref, h_out_ref):
    acc = x_ref[...]
    for j, o_ref in enumerate((oa_ref, ob_ref, oc_ref, od_ref)):
        acc = acc + _mm(o_ref[...], w_ref[j])
    x_out_ref[...] = acc
    h = acc * lax.rsqrt(jnp.mean(acc * acc, axis=-1, keepdims=True) + NORM_EPS) * g_ref[...]
    h_out_ref[...] = h.astype(h_out_ref.dtype)


def _out_proj(x, outs, w_out, gain, rows=512):
    n, d = x.shape
    xs = pl.BlockSpec((rows, d), lambda i: (i, 0))
    os_ = pl.BlockSpec((rows, GROUP), lambda i: (i, 0))
    return pl.pallas_call(
        _out_proj_kernel,
        grid=(n // rows,),
        in_specs=[xs, os_, os_, os_, os_, _full((4, GROUP, d)), _full((1, d))],
        out_specs=[xs, xs],
        out_shape=[jax.ShapeDtypeStruct((n, d), F32), jax.ShapeDtypeStruct((n, d), BF16)],
        compiler_params=_params("parallel"),
        name="out_proj",
    )(x, *outs, w_out, gain.reshape(1, d))


def _top_values(s, count, want_rank=False):
    tops = []
    rank = jnp.full(s.shape, float(count), F32) if want_rank else None
    for r in range(count):
        m = jnp.max(s, axis=0, keepdims=True)
        hit = s == m
        tops.append(m)
        if want_rank:
            rank = jnp.where(hit, float(r), rank)
        s = jnp.where(hit, -jnp.inf, s)
    return jnp.concatenate(tops, axis=0), rank


def _pair_candidates(a, b):
    k = PEER_TOPK
    rows8 = lax.broadcasted_iota(jnp.int32, (SUBLANE, 1), 0)
    parts = [a[0:1, :] + b, a[1:2, :] + b[:SUBLANE]]
    for r in range(2, SUBLANE):
        parts.append(jnp.where(rows8 < k // (r + 1), a[r:r + 1, :] + b[:SUBLANE], -jnp.inf))
    parts.append(a[SUBLANE:, :] + b[0:1, :])
    return jnp.concatenate(parts, axis=0)


def _route_kernel(h_ref, wq_ref, keys_ref, m_ref, r2_ref, e1_ref, e2_ref, sc_ref):
    q = _mm(h_ref[...], wq_ref[...])
    for j in range(2 * PEER_HEADS):
        sc_ref[j] = _mm(keys_ref[j], q[:, j * PEER_HALF:(j + 1) * PEER_HALF], NT)

    def per_head(h, carry):
        s1, s2 = sc_ref[2 * h], sc_ref[2 * h + 1]
        a, _ = _top_values(s1, PEER_TOPK)
        b, rank2 = _top_values(s2, PEER_TOPK, want_rank=True)
        best, _ = _top_values(_pair_candidates(a, b), PEER_TOPK)
        tau = best[PEER_TOPK - 1:PEER_TOPK, :]
        z = jnp.sum(jnp.exp(best - best[0:1, :]), axis=0, keepdims=True)
        count = jnp.zeros(s1.shape, F32)
        for r in range(PEER_TOPK):
            reach = jnp.sum(jnp.where(a[r:r + 1, :] + b >= tau, 1.0, 0.0), axis=0, keepdims=True)
            count = jnp.where(s1 == a[r:r + 1, :], reach, count)
        m_ref[h] = count
        r2_ref[h] = rank2.astype(BF16)
        e1_ref[h] = jnp.exp(s1 - a[0:1, :]) * (0.5 / z)
        e2_ref[h] = jnp.exp(s2 - b[0:1, :]).astype(BF16)
        return carry

    lax.fori_loop(0, PEER_HEADS, per_head, 0)


def _peer_route(h2, wq, keys):
    n, d = h2.shape
    t = ROUTE_ROWS
    big = pl.BlockSpec((PEER_HEADS, PEER_KEYS, t), lambda i: (0, 0, i))
    big_shape = jax.ShapeDtypeStruct((PEER_HEADS, PEER_KEYS, n), F32)
    return pl.pallas_call(
        _route_kernel,
        grid=(n // t,),
        in_specs=[pl.BlockSpec((t, d), lambda i: (i, 0)), _full(wq.shape), _full(keys.shape)],
        out_specs=[big, big, big, big],
        out_shape=[big_shape, jax.ShapeDtypeStruct(big_shape.shape, BF16), big_shape,
                   jax.ShapeDtypeStruct(big_shape.shape, BF16)],
        scratch_shapes=[pltpu.VMEM((2 * PEER_HEADS, PEER_KEYS, t), F32)],
        compiler_params=_params("parallel"),
        name="peer_route",
    )(h2, wq, keys)


def _peer_step(j, act_refs, gated_refs, ht_ref, u_ref, vt_ref, m_ref, r2_ref, e1_ref, e2_ref, acc_ref,
               mrow_ref, e1row_ref):
    blocks_per_step = PEER_TE // PEER_KEYS
    blocks_per_sub = PEER_SUB // PEER_KEYS
    half_cols = PEER_TN // 2
    packed = 2 * SUBLANE
    first_block = jnp.maximum(j - 1, 0) * blocks_per_step

    def gate_group(unit, q, i, k, s):
        cols = slice(q * half_cols + s * LANE, q * half_cols + (s + 1) * LANE)
        groups = range(PEER_KEYS // packed)
        w = [jnp.zeros((packed, LANE), BF16) for _ in groups]
        for h in range(PEER_HEADS):
            count = mrow_ref[i * PEER_HEADS + h, :, cols]
            e1 = e1row_ref[i * PEER_HEADS + h, :, cols]
            for g in groups:
                keys = slice(g * packed, (g + 1) * packed)
                w[g] = jnp.where(r2_ref[h, keys, cols] < count, w[g] + e1 * e2_ref[h, keys, cols], w[g])
        for g in groups:
            rows = slice(k * PEER_KEYS + g * packed, k * PEER_KEYS + (g + 1) * packed)
            x = act_refs[unit][rows, s * LANE:(s + 1) * LANE]
            inner = x * (_GELU_C1 + _GELU_C3 * (x * x))
            gated_refs[unit][rows, s * LANE:(s + 1) * LANE] = (x * (1.0 + jnp.tanh(inner))).astype(BF16) * w[g]

    for i in range(blocks_per_step):
        for h in range(PEER_HEADS):
            for src, dst in ((m_ref, mrow_ref), (e1_ref, e1row_ref)):
                row = src[h, pl.ds(first_block + i, 1), :]
                dst[i * PEER_HEADS + h] = jnp.broadcast_to(row, (packed, PEER_TN)).astype(BF16)
    for sub in range(PEER_TE // PEER_SUB):
        experts = slice(sub * PEER_SUB, (sub + 1) * PEER_SUB)
        for q in range(2):
            unit = 2 * sub + q
            for k in range(blocks_per_sub):
                for s in range(half_cols // LANE):
                    gate_group(unit, q, sub * blocks_per_sub + k, k, s)
            qcols = slice(q * half_cols, (q + 1) * half_cols)
            acc_ref[:, qcols] += _mm(vt_ref[:, experts], gated_refs[unit][...])
            act_refs[unit][...] = _mm(u_ref[experts, :], ht_ref[:, qcols])


def _peer_kernel(final, ht_ref, x_ref, u_ref, vt_ref, m_ref, r2_ref, e1_ref, e2_ref, g_ref, o_ref,
                 acc_ref, mrow_ref, e1row_ref, *unit_refs):
    j = pl.program_id(1)
    last = pl.num_programs(1) - 1
    act_refs, gated_refs = unit_refs[:len(unit_refs) // 2], unit_refs[len(unit_refs) // 2:]

    @pl.when(j == 0)
    def _():
        acc_ref[...] = jnp.zeros_like(acc_ref)
        for act_ref in act_refs:
            act_ref[...] = jnp.zeros_like(act_ref)

    _peer_step(j, act_refs, gated_refs, ht_ref=ht_ref, u_ref=u_ref, vt_ref=vt_ref, m_ref=m_ref, r2_ref=r2_ref,
               e1_ref=e1_ref, e2_ref=e2_ref, acc_ref=acc_ref, mrow_ref=mrow_ref, e1row_ref=e1row_ref)

    @pl.when(j == last)
    def _():
        y = x_ref[...] + acc_ref[...].T
        if final:
            y = y * lax.rsqrt(jnp.mean(y * y, axis=-1, keepdims=True) + NORM_EPS) * g_ref[...]
        o_ref[...] = y


def _peer_experts(h2t, x, u, vt, route, gain, final):
    n, d = x.shape
    tiles = u.shape[0] // PEER_TE
    row_tiles = (PEER_TE // PEER_KEYS) * PEER_HEADS
    units = 2 * (PEER_TE // PEER_SUB)
    once = pl.Buffered(1)
    big = pl.BlockSpec((PEER_HEADS, PEER_KEYS, PEER_TN), lambda i, j: (0, 0, i), pipeline_mode=once)
    tok = pl.BlockSpec((PEER_TN, d), lambda i, j: (i, 0), pipeline_mode=once)
    return pl.pallas_call(
        functools.partial(_peer_kernel, final),
        grid=(n // PEER_TN, tiles + 1),
        in_specs=[pl.BlockSpec((d, PEER_TN), lambda i, j: (0, i), pipeline_mode=once), tok,
                  pl.BlockSpec((PEER_TE, d), lambda i, j: (jnp.minimum(j, tiles - 1), 0)),
                  pl.BlockSpec((None, d, PEER_TE), lambda i, j: (jnp.maximum(j - 1, 0), 0, 0)),
                  big, big, big, big, _full((1, d))],
        out_specs=pl.BlockSpec((PEER_TN, d), lambda i, j: (i, 0)),
        out_shape=jax.ShapeDtypeStruct((n, d), F32),
        scratch_shapes=[pltpu.VMEM((d, PEER_TN), F32),
                        pltpu.VMEM((row_tiles, 2 * SUBLANE, PEER_TN), BF16),
                        pltpu.VMEM((row_tiles, 2 * SUBLANE, PEER_TN), BF16)]
        + [pltpu.VMEM((PEER_SUB, PEER_TN // 2), F32)] * units
        + [pltpu.VMEM((PEER_SUB, PEER_TN // 2), BF16)] * units,
        compiler_params=_params("parallel", "arbitrary"),
        name="peer_experts",
    )(h2t, x, u, vt, *route, gain.reshape(1, d))


def _pad_cols(w, width):
    return jnp.pad(w, ((0, 0), (0, width - w.shape[1])))


def _pad_rows(w, height):
    return jnp.pad(w, ((0, height - w.shape[0]), (0, 0)))


def _split_cols(w, widths):
    return jnp.split(w, np.cumsum(widths)[:-1].tolist(), axis=1)


def _block_diag(w):
    nb, bi, bo = w.shape
    eye = jnp.eye(nb, dtype=w.dtype)
    return (eye[:, None, :, None] * w[:, :, None, :]).reshape(nb * bi, nb * bo)


def kernel(x, norm_mix, w_in, w_out, rw_mu, rw_w0, rw_w_up, rw_a0, rw_a_up, rw_g_up, rw_k_k, rw_k_a, rw_r_k, rw_ln_w, rw_ln_b, rw_v0, rw_v_down, rw_v_up, gla_a_up, gla_a_bias, gla_norm, lru_conv_w, lru_conv_b, lru_w_r, lru_b_r, lru_w_i, lru_b_i, lru_lam, gdn_conv_w, gdn_a_log, gdn_dt_bias, gdn_norm, norm_ffn, peer_wq, peer_sub_keys, peer_u, peer_v, norm_final):
    batch, seq, d = x.shape
    depth = w_in.shape[0]
    n = batch * seq
    xf = x.reshape(n, d)
    hk = GLA_HEADS * GLA_DK
    rw_widths = (GROUP, GROUP, GROUP, 64, 64, 128)
    gla_widths = (hk, hk, GROUP, GROUP, 16)
    gdn_widths = (GROUP, GROUP, GROUP, GROUP, GDN_HEADS, GDN_HEADS)
    rw_cols, gla_cols, lru_cols = sum(rw_widths), sum(gla_widths), 2 * GROUP
    row = lambda a: a.reshape(1, -1)
    v_first = None
    for l in range(depth):
        wa, wb, wc, wd = _split_cols(w_in[l], (rw_cols, gla_cols, lru_cols, w_in.shape[2] - rw_cols - gla_cols - lru_cols))
        pr = _split_cols(wa, rw_widths)
        wa = jnp.concatenate(pr[:3] + [_pad_cols(t, LANE) for t in pr[3:]], axis=1).astype(BF16)
        mu = _split_cols(row(rw_mu[l]), rw_widths)
        mu = jnp.concatenate(mu[:3] + [_pad_cols(t, LANE) for t in mu[3:]], axis=1)
        pg = _split_cols(wb, gla_widths)
        wb = jnp.concatenate(pg[:4] + [_pad_cols(pg[4], LANE)], axis=1).astype(BF16)
        pdn = _split_cols(wd, gdn_widths)
        wd = jnp.concatenate(pdn[:4] + [_pad_cols(jnp.concatenate(pdn[4:], axis=1), LANE)], axis=1).astype(BF16)

        seqs = lambda t: t.reshape(batch, seq, t.shape[-1])
        flat = lambda t: t.reshape(n, t.shape[-1])
        pa, pb, pc, pd_ = _norm_matmul(xf, norm_mix[l], (wa, wb, wc.astype(BF16), wd))
        pa, pb, pd_ = seqs(pa), seqs(pb), seqs(pd_)

        rw_wts = (mu, row(rw_w0[l]), _split_weight(_pad_rows(rw_w_up[l], LANE)), row(rw_a0[l]),
                  _split_weight(_pad_rows(rw_a_up[l], LANE)), _split_weight(rw_g_up[l]), row(rw_k_k[l]),
                  row(rw_k_a[l]), row(rw_r_k[l]), row(rw_ln_w[l]), row(rw_ln_b[l]))
        vres = None if l == 0 else (row(rw_v0[l - 1]), _split_weight(_pad_cols(rw_v_down[l - 1], LANE)),
                                    _split_weight(_pad_rows(rw_v_up[l - 1], LANE)))
        oa, v_first = _rwkv_mixer(pa, v_first, batch, seq, rw_wts, vres)
        ob = _gla_mixer(pb, batch, seq, (_pad_rows(gla_a_up[l], LANE), row(gla_a_bias[l]),
                                         row(jnp.tile(gla_norm[l], GLA_HEADS))))
        oc = _lru_mixer(pc, batch, seq, (lru_conv_w[l], row(lru_conv_b[l]), _block_diag(lru_w_r[l]).astype(BF16),
                                         row(lru_b_r[l]), _block_diag(lru_w_i[l]).astype(BF16), row(lru_b_i[l]),
                                         row(lru_lam[l])))
        head_lanes = jnp.zeros((1, LANE), F32)
        od = _gdn_mixer(pd_, batch, seq, (gdn_conv_w[l],
                                          head_lanes.at[0, GDN_HEADS:2 * GDN_HEADS].set(gdn_a_log[l]),
                                          head_lanes.at[0, GDN_HEADS:2 * GDN_HEADS].set(gdn_dt_bias[l]),
                                          row(gdn_norm[l])))
        xf, h2 = _out_proj(xf, (flat(oa), flat(ob), oc, flat(od)), w_out[l].reshape(4, GROUP, d).astype(BF16),
                           norm_ffn[l])
        route = _peer_route(h2, peer_wq[l].astype(BF16),
                            peer_sub_keys[l].reshape(2 * PEER_HEADS, PEER_KEYS, PEER_HALF).astype(BF16))
        vt = peer_v[l].reshape(-1, PEER_TE, d).transpose(0, 2, 1).astype(BF16)
        xf = _peer_experts(h2.T, xf, peer_u[l].astype(BF16), vt, route, norm_final, final=(l == depth - 1))
    return xf.reshape(batch, seq, d)
```

```python
import functools
import math

import jax
import jax.numpy as jnp
import numpy as np
from jax import lax
from jax.experimental import pallas as pl
from jax.experimental.pallas import tpu as pltpu

F32 = jnp.float32
BF16 = jnp.bfloat16

NORM_EPS = 1e-6
GROUP = 512
CHUNK = 64
LANE = 128
SUBLANE = 8
VMEM_LIMIT = 56 * 1024 * 1024

RW_HEADS, RW_HEAD = 8, 64
RW_GN_EPS = 64e-5
RW_PAD_COLS = 3 * GROUP + 3 * LANE
GLA_HEADS, GLA_DK, GLA_DV = 4, 64, 128
GLA_NORMALIZER = 16.0
GLA_PAD_COLS = 2 * GLA_HEADS * GLA_DK + 2 * GROUP + LANE
LRU_C = 8.0
LRU_ROWS = 256
LRU_PAD_COLS = 2 * GROUP
GDN_HEADS, GDN_D = 4, 128
GDN_PAD_COLS = 4 * GROUP + LANE
CONV_W = 4
PEER_HEADS, PEER_KEYS, PEER_TOPK = 8, 128, 16
PEER_HALF = 128
ROUTE_ROWS = 512
PEER_TN = 512
PEER_TE = 1024
PEER_SUB = 512

NT = (((1,), (1,)), ((), ()))
NN = (((1,), (0,)), ((), ()))


def _mm(a, b, dims=NN):
    return lax.dot_general(a.astype(BF16), b.astype(BF16), dims, preferred_element_type=F32)


def _split2(x):
    hi = x.astype(BF16)
    return hi, (x - hi.astype(F32)).astype(BF16)


def _mm_mask_rhs(a, mask):
    hi, lo = _split2(a)
    n = a.shape[0]
    out = _mm(jnp.concatenate([hi, lo], axis=0), mask)
    return out[:n] + out[n:]


def _mm_mask_lhs(mask, b):
    hi = b.astype(BF16)
    rest = b - hi.astype(F32)
    mid = rest.astype(BF16)
    lo = (rest - mid.astype(F32)).astype(BF16)
    w = b.shape[1]
    out = _mm(mask, jnp.concatenate([hi, mid, lo], axis=1))
    return out[:, :w] + out[:, w:2 * w] + out[:, 2 * w:]


def _mm3(a, b_ref):
    a_hi, a_lo = _split2(a)
    b_hi, b_lo = b_ref[0], b_ref[1]
    return _mm(a_hi, b_hi) + (_mm(a_lo, b_hi) + _mm(a_hi, b_lo))


def _sigmoid(x):
    return 1.0 / (1.0 + jnp.exp(-x))


def _softplus(x):
    return jnp.maximum(x, 0.0) + jnp.log(1.0 + jnp.exp(-jnp.abs(x)))


def _gelu(x):
    return 0.5 * x * (1.0 + jnp.tanh(math.sqrt(2.0 / math.pi) * (x + 0.044715 * (x * x * x))))


_GELU_C1 = math.sqrt(2.0 / math.pi)
_GELU_C3 = 0.044715 * _GELU_C1


def _silu(x):
    return x * _sigmoid(x)


def _tile_rows(x, n):
    return jnp.concatenate([x] * n, axis=0)


def _full(shape):
    return pl.BlockSpec(shape, lambda *_: (0,) * len(shape))


def _seq_spec(batch, rows, cols):
    return pl.BlockSpec((batch, rows, cols), lambda i: (0, i, 0))


def _params(*sem):
    return pltpu.CompilerParams(dimension_semantics=sem, vmem_limit_bytes=VMEM_LIMIT)


def _neumann_inverse(n_wide, eye_wide, product):
    power = n_wide
    inv = eye_wide + n_wide
    for _ in range(int(math.log2(CHUNK)) - 1):
        power = product(power, power)
        yield
        inv = inv + product(inv, power)
        yield
    return inv


def _round_robin(chains):
    chains = list(chains)
    while chains:
        alive = []
        for chain in chains:
            try:
                next(chain)
                alive.append(chain)
            except StopIteration:
                pass
        chains = alive


def _norm_matmul_kernel(x_ref, g_ref, *refs):
    w_refs, o_refs = refs[:len(refs) // 2], refs[len(refs) // 2:]
    x = x_ref[...]
    h = (x * lax.rsqrt(jnp.mean(x * x, axis=-1, keepdims=True) + NORM_EPS) * g_ref[...]).astype(BF16)
    for w_ref, o_ref in zip(w_refs, o_refs):
        o_ref[...] = _mm(h, w_ref[...])


def _norm_matmul(x, gain, weights, rows=256):
    n, d = x.shape
    resident = pl.Buffered(1)
    return pl.pallas_call(
        _norm_matmul_kernel,
        grid=(n // rows,),
        in_specs=[pl.BlockSpec((rows, d), lambda i: (i, 0)), _full((1, d))]
        + [pl.BlockSpec(w.shape, lambda i: (0, 0), pipeline_mode=resident) for w in weights],
        out_specs=[pl.BlockSpec((rows, w.shape[1]), lambda i: (i, 0)) for w in weights],
        out_shape=[jax.ShapeDtypeStruct((n, w.shape[1]), F32) for w in weights],
        compiler_params=_params("parallel"),
        name="norm_in_proj",
    )(x, gain.reshape(1, d), *weights)


def _rwkv_kernel(has_vres, *refs):
    p_ref, s_ref, last_ref = refs[0], refs[-2], refs[-1]

    @pl.when(pl.program_id(0) == 0)
    def _():
        s_ref[...] = jnp.zeros_like(s_ref)
        last_ref[...] = jnp.zeros_like(last_ref)

    _round_robin(_rwkv_chunk(has_vres, b, refs) for b in range(p_ref.shape[0]))


def _rwkv_chunk(has_vres, b, refs):
    if has_vres:
        (p_ref, vf_ref, mu_ref, w0_ref, wup_ref, a0_ref, aup_ref, gup_ref, kk_ref, ka_ref, rk_ref, lnw_ref, lnb_ref,
         v0_ref, vdn_ref, vup_ref, hm_ref, hmf_ref, tril_ref, strict_ref, incl_ref, eye_ref,
         o_ref, s_ref, last_ref) = refs
    else:
        (p_ref, mu_ref, w0_ref, wup_ref, a0_ref, aup_ref, gup_ref, kk_ref, ka_ref, rk_ref, lnw_ref, lnb_ref,
         hm_ref, hmf_ref, tril_ref, strict_ref, incl_ref, eye_ref,
         o_ref, vf_out_ref, s_ref, last_ref) = refs
    c = CHUNK
    g3 = 3 * GROUP
    p = p_ref[b]
    rows = lax.broadcasted_iota(jnp.int32, (c, 1), 0)
    shifted = jnp.where(rows == 0, last_ref[b], pltpu.roll(p, 1, axis=0))
    last_ref[b] = p[c - 1:c, :]
    z = p + (shifted - p) * mu_ref[...]
    r, k, v = z[:, :GROUP], z[:, GROUP:2 * GROUP], z[:, 2 * GROUP:g3]
    wd, ad, gd = z[:, g3:g3 + LANE], z[:, g3 + LANE:g3 + 2 * LANE], z[:, g3 + 2 * LANE:]
    w_raw = -_softplus(-(w0_ref[...] + _mm3(jnp.tanh(wd), wup_ref))) - 0.5
    alr = _sigmoid(a0_ref[...] + _mm3(ad, aup_ref))
    gate = _mm3(_sigmoid(gd), gup_ref)
    yield
    if has_vres:
        low = _mm3(v, vdn_ref)
        yield
        mix = _sigmoid(v0_ref[...] + _mm3(low, vup_ref))
        v = v + (vf_ref[b] - v) * mix
    else:
        vf_out_ref[b] = v
    hm = hm_ref[...]
    pairs = [slice(i * LANE, (i + 1) * LANE) for i in range(GROUP // LANE)]

    def head_mm(x, y, dims=NN):
        return jnp.concatenate([_mm(x[:, sl], _tile_rows(y[:, sl].astype(BF16), 2) * hm, dims) for sl in pairs],
                               axis=1)

    def head_sum(x):
        return jnp.concatenate([_mm_mask_rhs(x[:, sl], hm) for sl in pairs], axis=1)

    kk = k * kk_ref[...]
    kk = kk * lax.rsqrt(jnp.maximum(head_sum(kk * kk), 1e-24))
    k = k * (1.0 + (alr - 1.0) * ka_ref[...])
    a_vec, b_vec = -kk, kk * alr
    logw = -jnp.exp(w_raw)

    cum = _mm_mask_lhs(tril_ref[...], logw)
    yield
    cum_last = cum[c - 1:c, :]
    e_neg = jnp.exp(-cum)
    e_end = jnp.exp(cum_last - cum)
    a_t = a_vec * jnp.exp(cum - logw)
    r_t = r * jnp.exp(cum)
    b_t, k_t = b_vec * e_neg, k * e_neg
    b_d, k_d = b_vec * e_end, k * e_end

    strict, incl = strict_ref[...], incl_ref[...]
    lhs = jnp.concatenate([a_t, r_t], axis=0)
    scores_b = head_mm(lhs, b_t, NT)
    scores_k = head_mm(lhs, k_t, NT)
    yield
    a_ab, a_ak = scores_b[:c] * strict, scores_k[:c] * strict
    a_rb, a_rk = scores_b[c:] * incl, scores_k[c:] * incl
    inv = yield from _neumann_inverse(a_ab, eye_ref[...], head_mm)

    ars = jnp.concatenate([_mm(lhs[:, sl], s_ref[b, i], NT) for i, sl in enumerate(pairs)], axis=1)
    rhs = ars[:c] + head_mm(a_ak, v)
    yield
    u = head_mm(inv, rhs)
    yield
    y = ars[c:] + head_mm(a_rb, u) + head_mm(a_rk, v)
    uv = jnp.concatenate([u, v], axis=0)
    bk = jnp.concatenate([b_d, k_d], axis=0)
    decay_all = jnp.exp(cum_last)
    for i, sl in enumerate(pairs):
        s_ref[b, i] = s_ref[b, i] * decay_all[:, sl] + _mm(uv[:, sl].T, bk[:, sl]) * hmf_ref[...]
    yield

    inv_n = 1.0 / RW_HEAD
    mean = head_sum(y) * inv_n
    yield
    d = y - mean
    var = head_sum(d * d) * inv_n
    yield
    yn = d * lax.rsqrt(var + RW_GN_EPS) * lnw_ref[...] + lnb_ref[...]
    bonus = head_sum(r * k * rk_ref[...]) * v
    o_ref[b] = ((yn + bonus) * gate).astype(o_ref.dtype)


def _rwkv_consts():
    c = CHUNK
    idx = np.arange(LANE)
    hm = (idx[:, None] // RW_HEAD == idx[None, :] // RW_HEAD).astype(np.float32)
    i = np.arange(c)[:, None]
    j = (np.arange(GROUP) % c)[None, :]
    tril = (np.arange(c)[:, None] >= np.arange(c)[None, :]).astype(np.float32)
    return (jnp.asarray(hm, BF16), jnp.asarray(hm), jnp.asarray(tril, BF16), jnp.asarray((i > j).astype(np.float32)),
            jnp.asarray((i >= j).astype(np.float32)), jnp.asarray((i == j).astype(np.float32)))


def _split_weight(w):
    hi = w.astype(BF16)
    return jnp.stack([hi, (w - hi.astype(F32)).astype(BF16)])


def _rwkv_mixer(p, v_first, batch, seq, wts, vres):
    c = CHUNK
    consts = _rwkv_consts()
    vec = _full((1, GROUP))
    up = _full((2, LANE, GROUP))
    small = [_full((1, RW_PAD_COLS)), vec, up, vec, up, up, vec, vec, vec, vec, vec]
    const_specs = [_full((LANE, LANE)), _full((LANE, LANE)), _full((c, c)), _full((c, GROUP)), _full((c, GROUP)),
                   _full((c, GROUP))]
    scratch = [pltpu.VMEM((batch, GROUP // LANE, LANE, LANE), F32), pltpu.VMEM((batch, 1, RW_PAD_COLS), F32)]
    p_spec = _seq_spec(batch, c, RW_PAD_COLS)
    o_spec = _seq_spec(batch, c, GROUP)
    if vres is None:
        out, v_first = pl.pallas_call(
            functools.partial(_rwkv_kernel, False),
            grid=(seq // c,),
            in_specs=[p_spec] + small + const_specs,
            out_specs=[o_spec, o_spec],
            out_shape=[jax.ShapeDtypeStruct((batch, seq, GROUP), BF16), jax.ShapeDtypeStruct((batch, seq, GROUP), F32)],
            scratch_shapes=scratch,
            compiler_params=_params("arbitrary"),
            name="rwkv7_first",
        )(p, *wts, *consts)
        return out, v_first
    out = pl.pallas_call(
        functools.partial(_rwkv_kernel, True),
        grid=(seq // c,),
        in_specs=[p_spec, o_spec] + small + [vec, _full((2, GROUP, LANE)), up] + const_specs,
        out_specs=o_spec,
        out_shape=jax.ShapeDtypeStruct((batch, seq, GROUP), BF16),
        scratch_shapes=scratch,
        compiler_params=_params("arbitrary"),
        name="rwkv7_later",
    )(p, v_first, *wts, *vres, *consts)
    return out, v_first


def _gla_kernel(p_ref, aup_ref, abias_ref, gain_ref, hmk_ref, hmv_ref, hmvk_ref, hm128_ref, tril_ref, incl_ref,
                o_ref, s_ref):
    @pl.when(pl.program_id(0) == 0)
    def _():
        s_ref[...] = jnp.zeros_like(s_ref)

    _round_robin(_gla_chunk(b, p_ref, aup_ref, abias_ref, gain_ref, hmk_ref, hmv_ref, hmvk_ref, hm128_ref, tril_ref,
                            incl_ref, o_ref, s_ref) for b in range(p_ref.shape[0]))


def _gla_chunk(seq_id, p_ref, aup_ref, abias_ref, gain_ref, hmk_ref, hmv_ref, hmvk_ref, hm128_ref, tril_ref, incl_ref,
               o_ref, s_ref):
    c = CHUNK
    hk = GLA_HEADS * GLA_DK
    p = p_ref[seq_id]
    q, k = p[:, :hk] * (GLA_DK ** -0.5), p[:, hk:2 * hk]
    v, g = p[:, 2 * hk:2 * hk + GROUP], p[:, 2 * hk + GROUP:2 * hk + 2 * GROUP]
    ad = p[:, 2 * hk + 2 * GROUP:]
    log_a = -_softplus(-(_mm(ad, aup_ref[...]) + abias_ref[...])) * (1.0 / GLA_NORMALIZER)
    yield
    b = _mm_mask_lhs(tril_ref[...], log_a)
    yield
    b_ref = b[c // 2 - 1:c // 2, :]
    b_last = b[c - 1:c, :]
    k_near = (k * jnp.exp(b_ref - b)).astype(BF16)
    scores = _mm(q * jnp.exp(b - b_ref), _tile_rows(k_near, GLA_HEADS) * hmk_ref[...], NT)
    yield
    o = _mm(scores * incl_ref[...], _tile_rows(v.astype(BF16), GLA_HEADS) * hmv_ref[...])
    state = s_ref[seq_id]
    o = o + _mm(q * jnp.exp(b), state, NT)
    s_ref[seq_id] = state * jnp.exp(b_last) + _mm(v.T, k * jnp.exp(b_last - b)) * hmvk_ref[...]
    yield
    ms = _mm_mask_rhs(o * o, hm128_ref[...]) * (1.0 / GLA_DV)
    yield
    o = o * lax.rsqrt(ms + NORM_EPS) * gain_ref[...]
    o_ref[seq_id] = (o * _silu(g)).astype(o_ref.dtype)


def _gla_consts():
    c = CHUNK
    hk = GLA_HEADS * GLA_DK
    rk, rv = np.arange(hk), np.arange(GROUP)
    hmk = (rk[:, None] // c == rk[None, :] // GLA_DK).astype(np.float32)
    hmv = (rk[:, None] // c == rv[None, :] // GLA_DV).astype(np.float32)
    hmvk = (rv[:, None] // GLA_DV == rk[None, :] // GLA_DK).astype(np.float32)
    hm128 = (rv[:, None] // GLA_DV == rv[None, :] // GLA_DV).astype(np.float32)
    tril = (np.arange(c)[:, None] >= np.arange(c)[None, :]).astype(np.float32)
    incl = (np.arange(c)[:, None] >= (np.arange(GLA_HEADS * c) % c)[None, :]).astype(np.float32)
    return (jnp.asarray(hmk, BF16), jnp.asarray(hmv, BF16), jnp.asarray(hmvk), jnp.asarray(hm128, BF16),
            jnp.asarray(tril, BF16), jnp.asarray(incl))


def _gla_mixer(p, batch, seq, wts):
    c = CHUNK
    hk = GLA_HEADS * GLA_DK
    return pl.pallas_call(
        _gla_kernel,
        grid=(seq // c,),
        in_specs=[_seq_spec(batch, c, GLA_PAD_COLS), _full((LANE, hk)), _full((1, hk)), _full((1, GROUP)),
                  _full((GLA_HEADS * c, hk)), _full((GLA_HEADS * c, GROUP)), _full((GROUP, hk)),
                  _full((GROUP, GROUP)), _full((c, c)), _full((c, GLA_HEADS * c))],
        out_specs=_seq_spec(batch, c, GROUP),
        out_shape=jax.ShapeDtypeStruct((batch, seq, GROUP), BF16),
        scratch_shapes=[pltpu.VMEM((batch, GROUP, hk), F32)],
        compiler_params=_params("arbitrary"),
        name="gla",
    )(p, *wts, *_gla_consts())


def _causal_conv(x, hist_ref, w_ref):
    rows8 = lax.broadcasted_iota(jnp.int32, (SUBLANE, 1), 0)
    hist = hist_ref[...]
    y = x * w_ref[CONV_W - 1:CONV_W, :]
    for d in range(1, CONV_W):
        xr = pltpu.roll(x, d, axis=0)
        head = jnp.where(rows8 < d, pltpu.roll(hist, d, axis=0), xr[:SUBLANE])
        y = y + jnp.concatenate([head, xr[SUBLANE:]], axis=0) * w_ref[CONV_W - 1 - d:CONV_W - d, :]
    hist_ref[...] = x[x.shape[0] - SUBLANE:, :]
    return y


def _lru_kernel(p_ref, cw_ref, cb_ref, wr_ref, br_ref, wi_ref, bi_ref, lam_ref, o_ref, hist_ref, h_ref):
    t = LRU_ROWS

    @pl.when(pl.program_id(1) == 0)
    def _():
        hist_ref[...] = jnp.zeros_like(hist_ref)
        h_ref[...] = jnp.zeros_like(h_ref)

    p = p_ref[...]
    xc = _causal_conv(p[:, :GROUP], hist_ref, cw_ref) + cb_ref[...]
    r = _sigmoid(_mm(xc, wr_ref[...]) + br_ref[...])
    i = _sigmoid(_mm(xc, wi_ref[...]) + bi_ref[...])
    log_a = -LRU_C * r * _softplus(-lam_ref[...])
    u = xc * i * jnp.sqrt(1.0 - jnp.exp(2.0 * log_a))
    a = jnp.exp(log_a)
    rows = lax.broadcasted_iota(jnp.int32, (t, 1), 0)
    d = 1
    while d < t:
        keep = rows >= d
        a_s = jnp.where(keep, pltpu.roll(a, d, axis=0), 1.0)
        u_s = jnp.where(keep, pltpu.roll(u, d, axis=0), 0.0)
        u = a * u_s + u
        a = a * a_s
        d *= 2
    h = u + a * h_ref[...]
    h_ref[...] = h[t - 1:t, :]
    o_ref[...] = (h * _gelu(p[:, GROUP:])).astype(o_ref.dtype)


def _lru_mixer(p, batch, seq, wts):
    n = p.shape[0]
    t = LRU_ROWS
    nt = seq // t
    row = lambda b, i: (b * nt + i, 0)
    vec = _full((1, GROUP))
    return pl.pallas_call(
        _lru_kernel,
        grid=(batch, nt),
        in_specs=[pl.BlockSpec((t, LRU_PAD_COLS), row), _full((CONV_W, GROUP)), vec, _full((GROUP, GROUP)), vec,
                  _full((GROUP, GROUP)), vec, vec],
        out_specs=pl.BlockSpec((t, GROUP), row),
        out_shape=jax.ShapeDtypeStruct((n, GROUP), BF16),
        scratch_shapes=[pltpu.VMEM((SUBLANE, GROUP), F32), pltpu.VMEM((1, GROUP), F32)],
        compiler_params=_params("parallel", "arbitrary"),
        name="rglru",
    )(p, *wts)


def _gdn_kernel(p_ref, cw_ref, alog_ref, dtb_ref, gain_ref, tril_ref, o_ref, hist_ref, s_ref):
    @pl.when(pl.program_id(0) == 0)
    def _():
        hist_ref[...] = jnp.zeros_like(hist_ref)
        s_ref[...] = jnp.zeros_like(s_ref)

    c = CHUNK
    g3 = 3 * GROUP
    ri = lax.broadcasted_iota(jnp.int32, (c, c), 0)
    ci = lax.broadcasted_iota(jnp.int32, (c, c), 1)
    masks = (ri >= ci, ri > ci, (ri == ci).astype(F32))
    chains = []
    for b in range(p_ref.shape[0]):
        p = p_ref[b]
        qkv = _silu(_causal_conv(p[:, :g3], hist_ref.at[b], cw_ref))
        zg = p[:, g3:g3 + GROUP]
        ba = p[:, g3 + GROUP:]
        beta_all = _sigmoid(ba)
        g_all = -jnp.exp(alog_ref[...]) * _softplus(ba + dtb_ref[...])
        gc_all = _mm_mask_lhs(tril_ref[...], g_all)
        gc_rows = gc_all.T
        for h in range(GDN_HEADS):
            chains.append(_gdn_head(h, qkv, zg, beta_all, gc_all, gc_rows, masks, gain_ref, o_ref.at[b], s_ref.at[b]))
    _round_robin(chains)


def _gdn_head(h, qkv, zg, beta_all, gc_all, gc_rows, masks, gain_ref, o_ref, s_ref):
    c = CHUNK
    dh = GDN_D
    incl, strict, eye = masks
    sl = slice(h * dh, (h + 1) * dh)
    q = qkv[:, sl]
    k = qkv[:, GROUP + h * dh:GROUP + (h + 1) * dh]
    v = qkv[:, 2 * GROUP + h * dh:2 * GROUP + (h + 1) * dh]
    q = q * lax.rsqrt(jnp.sum(q * q, axis=-1, keepdims=True) + 1e-6) * (dh ** -0.5)
    k = k * lax.rsqrt(jnp.sum(k * k, axis=-1, keepdims=True) + 1e-6)
    beta = beta_all[:, h:h + 1]
    gc = gc_all[:, GDN_HEADS + h:GDN_HEADS + h + 1]
    gc_row = gc_rows[GDN_HEADS + h:GDN_HEADS + h + 1, :]
    gc_last = gc[c - 1:c, :]
    decay = jnp.where(incl, jnp.exp(jnp.minimum(gc - gc_row, 0.0)), 0.0)
    kb = k * beta
    lmat = jnp.where(strict, _mm(kb, k, NT) * decay, 0.0)
    attn = _mm(q, k, NT) * decay
    yield
    tinv = yield from _neumann_inverse(-lmat, eye, _mm)
    e_gc = jnp.exp(gc)
    u = _mm(tinv, v * beta)
    w = _mm(tinv, kb * e_gc)
    yield
    state = s_ref[h]
    v_new = u - _mm(w, state)
    yield
    o = _mm(q * e_gc, state) + _mm(attn, v_new)
    s_ref[h] = state * jnp.exp(gc_last) + _mm((k * jnp.exp(gc_last - gc)).T, v_new)
    yield
    o = o * lax.rsqrt(jnp.mean(o * o, axis=-1, keepdims=True) + NORM_EPS) * gain_ref[...]
    o_ref[:, sl] = (o * _silu(zg[:, sl])).astype(o_ref.dtype)


def _gdn_mixer(p, batch, seq, wts):
    c = CHUNK
    tril = jnp.asarray((np.arange(c)[:, None] >= np.arange(c)[None, :]).astype(np.float32), BF16)
    return pl.pallas_call(
        _gdn_kernel,
        grid=(seq // c,),
        in_specs=[_seq_spec(batch, c, GDN_PAD_COLS), _full((CONV_W, 3 * GROUP)), _full((1, LANE)),
                  _full((1, LANE)), _full((1, GDN_D)), _full((c, c))],
        out_specs=_seq_spec(batch, c, GROUP),
        out_shape=jax.ShapeDtypeStruct((batch, seq, GROUP), BF16),
        scratch_shapes=[pltpu.VMEM((batch, SUBLANE, 3 * GROUP), F32),
                        pltpu.VMEM((batch, GDN_HEADS, GDN_D, GDN_D), F32)],
        compiler_params=_params("arbitrary"),
        name="gdn",
    )(p, *wts, tril)


def _out_proj_kernel(x_ref, oa_ref, ob_ref, oc_ref, od_ref, w_ref, g_ref, x_out_ref, h_out_ref):
    acc = x_ref[...]
    for j, o_ref in enumerate((oa_ref, ob_ref, oc_ref, od_ref)):
        acc = acc + _mm(o_ref[...], w_ref[j])
    x_out_ref[...] = acc
    h = acc * lax.rsqrt(jnp.mean(acc * acc, axis=-1, keepdims=True) + NORM_EPS) * g_ref[...]
    h_out_ref[...] = h.astype(h_out_ref.dtype)


def _out_proj(x, outs, w_out, gain, rows=512):
    n, d = x.shape
    xs = pl.BlockSpec((rows, d), lambda i: (i, 0))
    os_ = pl.BlockSpec((rows, GROUP), lambda i: (i, 0))
    return pl.pallas_call(
        _out_proj_kernel,
        grid=(n // rows,),
        in_specs=[xs, os_, os_, os_, os_, _full((4, GROUP, d)), _full((1, d))],
        out_specs=[xs, xs],
        out_shape=[jax.ShapeDtypeStruct((n, d), F32), jax.ShapeDtypeStruct((n, d), BF16)],
        compiler_params=_params("parallel"),
        name="out_proj",
    )(x, *outs, w_out, gain.reshape(1, d))


def _top_values(s, count, want_rank=False):
    tops = []
    rank = jnp.full(s.shape, float(count), F32) if want_rank else None
    for r in range(count):
        m = jnp.max(s, axis=0, keepdims=True)
        hit = s == m
        tops.append(m)
        if want_rank:
            rank = jnp.where(hit, float(r), rank)
        s = jnp.where(hit, -jnp.inf, s)
    return jnp.concatenate(tops, axis=0), rank


def _pair_candidates(a, b):
    k = PEER_TOPK
    rows8 = lax.broadcasted_iota(jnp.int32, (SUBLANE, 1), 0)
    parts = [a[0:1, :] + b, a[1:2, :] + b[:SUBLANE]]
    for r in range(2, SUBLANE):
        parts.append(jnp.where(rows8 < k // (r + 1), a[r:r + 1, :] + b[:SUBLANE], -jnp.inf))
    parts.append(a[SUBLANE:, :] + b[0:1, :])
    return jnp.concatenate(parts, axis=0)


def _route_kernel(h_ref, wq_ref, keys_ref, m_ref, r2_ref, e1_ref, e2_ref, sc_ref):
    q = _mm(h_ref[...], wq_ref[...])
    for j in range(2 * PEER_HEADS):
        sc_ref[j] = _mm(keys_ref[j], q[:, j * PEER_HALF:(j + 1) * PEER_HALF], NT)

    def per_head(h, carry):
        s1, s2 = sc_ref[2 * h], sc_ref[2 * h + 1]
        a, _ = _top_values(s1, PEER_TOPK)
        b, rank2 = _top_values(s2, PEER_TOPK, want_rank=True)
        best, _ = _top_values(_pair_candidates(a, b), PEER_TOPK)
        tau = best[PEER_TOPK - 1:PEER_TOPK, :]
        z = jnp.sum(jnp.exp(best - best[0:1, :]), axis=0, keepdims=True)
        count = jnp.zeros(s1.shape, F32)
        for r in range(PEER_TOPK):
            reach = jnp.sum(jnp.where(a[r:r + 1, :] + b >= tau, 1.0, 0.0), axis=0, keepdims=True)
            count = jnp.where(s1 == a[r:r + 1, :], reach, count)
        m_ref[h] = count
        r2_ref[h] = rank2.astype(BF16)
        e1_ref[h] = jnp.exp(s1 - a[0:1, :]) * (0.5 / z)
        e2_ref[h] = jnp.exp(s2 - b[0:1, :]).astype(BF16)
        return carry

    lax.fori_loop(0, PEER_HEADS, per_head, 0)


def _peer_route(h2, wq, keys):
    n, d = h2.shape
    t = ROUTE_ROWS
    big = pl.BlockSpec((PEER_HEADS, PEER_KEYS, t), lambda i: (0, 0, i))
    big_shape = jax.ShapeDtypeStruct((PEER_HEADS, PEER_KEYS, n), F32)
    return pl.pallas_call(
        _route_kernel,
        grid=(n // t,),
        in_specs=[pl.BlockSpec((t, d), lambda i: (i, 0)), _full(wq.shape), _full(keys.shape)],
        out_specs=[big, big, big, big],
        out_shape=[big_shape, jax.ShapeDtypeStruct(big_shape.shape, BF16), big_shape,
                   jax.ShapeDtypeStruct(big_shape.shape, BF16)],
        scratch_shapes=[pltpu.VMEM((2 * PEER_HEADS, PEER_KEYS, t), F32)],
        compiler_params=_params("parallel"),
        name="peer_route",
    )(h2, wq, keys)


def _peer_step(j, act_refs, gated_refs, ht_ref, u_ref, vt_ref, m_ref, r2_ref, e1_ref, e2_ref, acc_ref,
               mrow_ref, e1row_ref):
    blocks_per_step = PEER_TE // PEER_KEYS
    blocks_per_sub = PEER_SUB // PEER_KEYS
    half_cols = PEER_TN // 2
    packed = 2 * SUBLANE
    first_block = jnp.maximum(j - 1, 0) * blocks_per_step

    def gate_group(unit, q, i, k, s):
        cols = slice(q * half_cols + s * LANE, q * half_cols + (s + 1) * LANE)
        groups = range(PEER_KEYS // packed)
        w = [jnp.zeros((packed, LANE), BF16) for _ in groups]
        for h in range(PEER_HEADS):
            count = mrow_ref[i * PEER_HEADS + h, :, cols]
            e1 = e1row_ref[i * PEER_HEADS + h, :, cols]
            for g in groups:
                keys = slice(g * packed, (g + 1) * packed)
                w[g] = jnp.where(r2_ref[h, keys, cols] < count, w[g] + e1 * e2_ref[h, keys, cols], w[g])
        for g in groups:
            rows = slice(k * PEER_KEYS + g * packed, k * PEER_KEYS + (g + 1) * packed)
            x = act_refs[unit][rows, s * LANE:(s + 1) * LANE]
            inner = x * (_GELU_C1 + _GELU_C3 * (x * x))
            gated_refs[unit][rows, s * LANE:(s + 1) * LANE] = (x * (1.0 + jnp.tanh(inner))).astype(BF16) * w[g]

    for i in range(blocks_per_step):
        for h in range(PEER_HEADS):
            for src, dst in ((m_ref, mrow_ref), (e1_ref, e1row_ref)):
                row = src[h, pl.ds(first_block + i, 1), :]
                dst[i * PEER_HEADS + h] = jnp.broadcast_to(row, (packed, PEER_TN)).astype(BF16)
    for sub in range(PEER_TE // PEER_SUB):
        experts = slice(sub * PEER_SUB, (sub + 1) * PEER_SUB)
        for q in range(2):
            unit = 2 * sub + q
            for k in range(blocks_per_sub):
                for s in range(half_cols // LANE):
                    gate_group(unit, q, sub * blocks_per_sub + k, k, s)
            qcols = slice(q * half_cols, (q + 1) * half_cols)
            acc_ref[:, qcols] += _mm(vt_ref[:, experts], gated_refs[unit][...])
            act_refs[unit][...] = _mm(u_ref[experts, :], ht_ref[:, qcols])


def _peer_kernel(final, ht_ref, x_ref, u_ref, vt_ref, m_ref, r2_ref, e1_ref, e2_ref, g_ref, o_ref,
                 acc_ref, mrow_ref, e1row_ref, *unit_refs):
    j = pl.program_id(1)
    last = pl.num_programs(1) - 1
    act_refs, gated_refs = unit_refs[:len(unit_refs) // 2], unit_refs[len(unit_refs) // 2:]

    @pl.when(j == 0)
    def _():
        acc_ref[...] = jnp.zeros_like(acc_ref)
        half_cols = PEER_TN // 2
        for unit, act_ref in enumerate(act_refs):
            sub, q = divmod(unit, 2)
            act_ref[...] = _mm(u_ref[sub * PEER_SUB:(sub + 1) * PEER_SUB, :], ht_ref[:, q * half_cols:(q + 1) * half_cols])

    @pl.when(j > 0)
    def _():
        _peer_step(j, act_refs, gated_refs, ht_ref=ht_ref, u_ref=u_ref, vt_ref=vt_ref, m_ref=m_ref, r2_ref=r2_ref,
                   e1_ref=e1_ref, e2_ref=e2_ref, acc_ref=acc_ref, mrow_ref=mrow_ref, e1row_ref=e1row_ref)

    @pl.when(j == last)
    def _():
        y = x_ref[...] + acc_ref[...].T
        if final:
            y = y * lax.rsqrt(jnp.mean(y * y, axis=-1, keepdims=True) + NORM_EPS) * g_ref[...]
        o_ref[...] = y


def _peer_experts(h2t, x, u, vt, route, gain, final):
    n, d = x.shape
    tiles = u.shape[0] // PEER_TE
    row_tiles = (PEER_TE // PEER_KEYS) * PEER_HEADS
    units = 2 * (PEER_TE // PEER_SUB)
    once = pl.Buffered(1)
    big = pl.BlockSpec((PEER_HEADS, PEER_KEYS, PEER_TN), lambda i, j: (0, 0, i), pipeline_mode=once)
    tok = pl.BlockSpec((PEER_TN, d), lambda i, j: (i, 0), pipeline_mode=once)
    return pl.pallas_call(
        functools.partial(_peer_kernel, final),
        grid=(n // PEER_TN, tiles + 1),
        in_specs=[pl.BlockSpec((d, PEER_TN), lambda i, j: (0, i), pipeline_mode=once), tok,
                  pl.BlockSpec((PEER_TE, d), lambda i, j: (jnp.minimum(j, tiles - 1), 0)),
                  pl.BlockSpec((None, d, PEER_TE), lambda i, j: (jnp.maximum(j - 1, 0), 0, 0)),
                  big, big, big, big, _full((1, d))],
        out_specs=pl.BlockSpec((PEER_TN, d), lambda i, j: (i, 0)),
        out_shape=jax.ShapeDtypeStruct((n, d), F32),
        scratch_shapes=[pltpu.VMEM((d, PEER_TN), F32),
                        pltpu.VMEM((row_tiles, 2 * SUBLANE, PEER_TN), BF16),
                        pltpu.VMEM((row_tiles, 2 * SUBLANE, PEER_TN), BF16)]
        + [pltpu.VMEM((PEER_SUB, PEER_TN // 2), F32)] * units
        + [pltpu.VMEM((PEER_SUB, PEER_TN // 2), BF16)] * units,
        compiler_params=_params("parallel", "arbitrary"),
        name="peer_experts",
    )(h2t, x, u, vt, *route, gain.reshape(1, d))


def _pad_cols(w, width):
    return jnp.pad(w, ((0, 0), (0, width - w.shape[1])))


def _pad_rows(w, height):
    return jnp.pad(w, ((0, height - w.shape[0]), (0, 0)))


def _split_cols(w, widths):
    return jnp.split(w, np.cumsum(widths)[:-1].tolist(), axis=1)


def _block_diag(w):
    nb, bi, bo = w.shape
    eye = jnp.eye(nb, dtype=w.dtype)
    return (eye[:, None, :, None] * w[:, :, None, :]).reshape(nb * bi, nb * bo)


def kernel(x, norm_mix, w_in, w_out, rw_mu, rw_w0, rw_w_up, rw_a0, rw_a_up, rw_g_up, rw_k_k, rw_k_a, rw_r_k, rw_ln_w, rw_ln_b, rw_v0, rw_v_down, rw_v_up, gla_a_up, gla_a_bias, gla_norm, lru_conv_w, lru_conv_b, lru_w_r, lru_b_r, lru_w_i, lru_b_i, lru_lam, gdn_conv_w, gdn_a_log, gdn_dt_bias, gdn_norm, norm_ffn, peer_wq, peer_sub_keys, peer_u, peer_v, norm_final):
    batch, seq, d = x.shape
    depth = w_in.shape[0]
    n = batch * seq
    xf = x.reshape(n, d)
    hk = GLA_HEADS * GLA_DK
    rw_widths = (GROUP, GROUP, GROUP, 64, 64, 128)
    gla_widths = (hk, hk, GROUP, GROUP, 16)
    gdn_widths = (GROUP, GROUP, GROUP, GROUP, GDN_HEADS, GDN_HEADS)
    rw_cols, gla_cols, lru_cols = sum(rw_widths), sum(gla_widths), 2 * GROUP
    row = lambda a: a.reshape(1, -1)
    v_first = None
    for l in range(depth):
        wa, wb, wc, wd = _split_cols(w_in[l], (rw_cols, gla_cols, lru_cols, w_in.shape[2] - rw_cols - gla_cols - lru_cols))
        pr = _split_cols(wa, rw_widths)
        wa = jnp.concatenate(pr[:3] + [_pad_cols(t, LANE) for t in pr[3:]], axis=1).astype(BF16)
        mu = _split_cols(row(rw_mu[l]), rw_widths)
        mu = jnp.concatenate(mu[:3] + [_pad_cols(t, LANE) for t in mu[3:]], axis=1)
        pg = _split_cols(wb, gla_widths)
        wb = jnp.concatenate(pg[:4] + [_pad_cols(pg[4], LANE)], axis=1).astype(BF16)
        pdn = _split_cols(wd, gdn_widths)
        wd = jnp.concatenate(pdn[:4] + [_pad_cols(jnp.concatenate(pdn[4:], axis=1), LANE)], axis=1).astype(BF16)

        seqs = lambda t: t.reshape(batch, seq, t.shape[-1])
        flat = lambda t: t.reshape(n, t.shape[-1])
        pa, pb, pc, pd_ = _norm_matmul(xf, norm_mix[l], (wa, wb, wc.astype(BF16), wd))
        pa, pb, pd_ = seqs(pa), seqs(pb), seqs(pd_)

        rw_wts = (mu, row(rw_w0[l]), _split_weight(_pad_rows(rw_w_up[l], LANE)), row(rw_a0[l]),
                  _split_weight(_pad_rows(rw_a_up[l], LANE)), _split_weight(rw_g_up[l]), row(rw_k_k[l]),
                  row(rw_k_a[l]), row(rw_r_k[l]), row(rw_ln_w[l]), row(rw_ln_b[l]))
        vres = None if l == 0 else (row(rw_v0[l - 1]), _split_weight(_pad_cols(rw_v_down[l - 1], LANE)),
                                    _split_weight(_pad_rows(rw_v_up[l - 1], LANE)))
        oa, v_first = _rwkv_mixer(pa, v_first, batch, seq, rw_wts, vres)
        ob = _gla_mixer(pb, batch, seq, (_pad_rows(gla_a_up[l], LANE), row(gla_a_bias[l]),
                                         row(jnp.tile(gla_norm[l], GLA_HEADS))))
        oc = _lru_mixer(pc, batch, seq, (lru_conv_w[l], row(lru_conv_b[l]), _block_diag(lru_w_r[l]).astype(BF16),
                                         row(lru_b_r[l]), _block_diag(lru_w_i[l]).astype(BF16), row(lru_b_i[l]),
                                         row(lru_lam[l])))
        head_lanes = jnp.zeros((1, LANE), F32)
        od = _gdn_mixer(pd_, batch, seq, (gdn_conv_w[l],
                                          head_lanes.at[0, GDN_HEADS:2 * GDN_HEADS].set(gdn_a_log[l]),
                                          head_lanes.at[0, GDN_HEADS:2 * GDN_HEADS].set(gdn_dt_bias[l]),
                                          row(gdn_norm[l])))
        xf, h2 = _out_proj(xf, (flat(oa), flat(ob), oc, flat(od)), w_out[l].reshape(4, GROUP, d).astype(BF16),
                           norm_ffn[l])
        route = _peer_route(h2, peer_wq[l].astype(BF16),
                            peer_sub_keys[l].reshape(2 * PEER_HEADS, PEER_KEYS, PEER_HALF).astype(BF16))
        vt = peer_v[l].reshape(-1, PEER_TE, d).transpose(0, 2, 1).astype(BF16)
        xf = _peer_experts(h2.T, xf, peer_u[l].astype(BF16), vt, route, norm_final, final=(l == depth - 1))
    return xf.reshape(batch, seq, d)
```
